```python
import math
import jax, jax.numpy as jnp
from jax import lax
import numpy as np

D_MODEL = 1024
BATCH = 8
SEQ = 4096
DEPTH = 1

EPS = 1e-6
MIX_WIDTH = D_MODEL
RET_DIM = 64
RET_WIDTH = MIX_WIDTH // 2
RET_HEADS = RET_WIDTH // RET_DIM
DIFF_DIM = 64
DIFF_WIDTH = MIX_WIDTH - RET_WIDTH
DIFF_HEADS = DIFF_WIDTH // (2 * DIFF_DIM)
IN_WIDTH = 4 * RET_WIDTH + 3 * DIFF_WIDTH
D_FF = 2816
N_BUCKETS = 32
MAX_DIST = 128
CHUNK = 128
Q_BLOCK = 128
ROPE_BASE = 10000.0
N_MOD = 9
NEG_INF = -1e30

kernel_name = "hybrid_retention_diffattn_macaron_adaln"


def rmsnorm(x, g):
    xf = x.astype(jnp.float32)
    y = xf * lax.rsqrt(jnp.mean(xf * xf, axis=-1, keepdims=True) + EPS)
    return (y * g.astype(jnp.float32)).astype(x.dtype)


def head_layernorm(x):
    xf = x.astype(jnp.float32)
    mu = jnp.mean(xf, axis=-1, keepdims=True)
    var = jnp.mean(jnp.square(xf - mu), axis=-1, keepdims=True)
    return ((xf - mu) * lax.rsqrt(var + EPS)).astype(x.dtype)


def modulate(h, shift, scale):
    return h * (1.0 + scale[:, None, :]) + shift[:, None, :]


def swiglu(h, w_in, w_down):
    g, u = jnp.split(h @ w_in, 2, axis=-1)
    return (jax.nn.silu(g) * u) @ w_down


def rope(x, pos):
    d = x.shape[-1]
    inv = ROPE_BASE ** (-jnp.arange(0, d, 2, dtype=jnp.float32) / d)
    ang = pos.astype(jnp.float32)[:, None] * inv[None, :]
    cos = jnp.cos(ang).astype(x.dtype)
    sin = jnp.sin(ang).astype(x.dtype)
    x1, x2 = x[..., : d // 2], x[..., d // 2:]
    return jnp.concatenate([x1 * cos - x2 * sin, x1 * sin + x2 * cos], axis=-1)


def t5_bucket(rel):
    n = jnp.maximum(rel, 0)
    max_exact = N_BUCKETS // 2
    nf = jnp.maximum(n, 1).astype(jnp.float32)
    large = max_exact + (jnp.log(nf / max_exact) / math.log(MAX_DIST / max_exact)
                         * (N_BUCKETS - max_exact)).astype(jnp.int32)
    large = jnp.minimum(large, N_BUCKETS - 1)
    return jnp.where(n < max_exact, n, large)


def retention(q, k, v):
    B, H, S, d = q.shape
    NC = S // CHUNK
    dt = q.dtype
    log_gamma = jnp.log1p(-(2.0 ** (-5.0 - jnp.arange(H, dtype=jnp.float32))))
    qc = q.reshape(B, H, NC, CHUNK, d)
    kc = k.reshape(B, H, NC, CHUNK, d)
    vc = v.reshape(B, H, NC, CHUNK, d)
    idx = jnp.arange(CHUNK, dtype=jnp.float32)
    dist = idx[:, None] - idx[None, :]
    decay_in = jnp.where(dist >= 0,
                         jnp.exp(log_gamma[:, None, None] * jnp.maximum(dist, 0.0)[None]),
                         0.0).astype(dt)
    s = jnp.einsum('bhncd,bhnkd->bhnck', qc, kc) * decay_in[None, :, None]
    intra = jnp.einsum('bhnck,bhnkd->bhncd', s, vc)
    zeta = jnp.exp(log_gamma[:, None] * (CHUNK - 1 - idx)[None]).astype(dt)
    kv = jnp.einsum('bhnkd,bhnke->bhnde', kc * zeta[None, :, None, :, None], vc)
    gamma_c = jnp.exp(log_gamma * CHUNK).astype(kv.dtype)[None, :, None, None]

    def step(R, kv_n):
        return R * gamma_c + kv_n, R

    R0 = jnp.zeros((B, H, d, d), dtype=kv.dtype)
    _, R_prev = lax.scan(step, R0, jnp.moveaxis(kv, 2, 0))
    R_prev = jnp.moveaxis(R_prev, 0, 2)
    xi = jnp.exp(log_gamma[:, None] * (idx + 1.0)[None]).astype(dt)
    cross = jnp.einsum('bhncd,bhnde->bhnce', qc, R_prev) * xi[None, :, None, :, None]
    return (intra + cross).reshape(B, H, S, d)


def diff_attention(q, k, v, lam, rel_bias):
    B, H, _, S, d = q.shape
    NB = S // Q_BLOCK
    scale = d ** -0.5
    kpos = jnp.arange(S)
    qb = q.reshape(B, H, 2, NB, Q_BLOCK, d)

    def block(i):
        qi = lax.dynamic_index_in_dim(qb, i, axis=3, keepdims=False)
        qpos = i * Q_BLOCK + jnp.arange(Q_BLOCK)
        rel = qpos[:, None] - kpos[None, :]
        bias = jnp.transpose(rel_bias[t5_bucket(rel)], (2, 0, 1)).astype(jnp.float32)
        logits = jnp.einsum('bhmqd,bhmkd->bhmqk', qi, k).astype(jnp.float32) * scale
        logits = logits + bias[None, :, None]
        logits = jnp.where(rel[None, None, None] >= 0, logits, NEG_INF)
        p = jax.nn.softmax(logits, axis=-1)
        a = p[:, :, 0] - lam * p[:, :, 1]
        return jnp.einsum('bhqk,bhke->bhqe', a.astype(v.dtype), v)

    out = lax.map(block, jnp.arange(NB))
    return jnp.moveaxis(out, 0, 2).reshape(B, H, S, 2 * d)


def token_mixer(h, w_in_l, lam, lambda_init, subln_g, grp_scale, w_out_l, rel_bias):
    B, S, _ = h.shape
    proj = h @ w_in_l
    splits = np.cumsum([RET_WIDTH] * 4 + [DIFF_WIDTH] * 2).tolist()
    rq, rk, rv, rg, dq, dk, dv = jnp.split(proj, splits, axis=-1)
    pos = jnp.arange(S)

    to_heads = lambda t: t.reshape(B, S, RET_HEADS, RET_DIM).transpose(0, 2, 1, 3)
    rq = rope(to_heads(rq), pos)
    rk = rope(to_heads(rk), pos) * (RET_DIM ** -0.5)
    y_ret = head_layernorm(retention(rq, rk, to_heads(rv)))
    y_ret = y_ret.transpose(0, 2, 1, 3).reshape(B, S, RET_WIDTH) * jax.nn.silu(rg)

    dq = dq.reshape(B, S, DIFF_HEADS, 2, DIFF_DIM).transpose(0, 2, 3, 1, 4)
    dk = dk.reshape(B, S, DIFF_HEADS, 2, DIFF_DIM).transpose(0, 2, 3, 1, 4)
    dv = dv.reshape(B, S, DIFF_HEADS, 2 * DIFF_DIM).transpose(0, 2, 1, 3)
    y_diff = rmsnorm(diff_attention(dq, dk, dv, lam, rel_bias), subln_g) * (1.0 - lambda_init)
    y_diff = y_diff.transpose(0, 2, 1, 3).reshape(B, S, DIFF_WIDTH)

    y = jnp.concatenate([y_ret, y_diff], axis=-1) * grp_scale
    return y @ w_out_l


def setup_inputs(seed: int = 0) -> dict:
    key = jax.random.key(seed)
    ks = jax.random.split(key, 24)
    nrm = lambda k, shape, s: jax.random.normal(k, shape, jnp.float32) * s
    gain = lambda k, shape: 1.0 + 0.05 * jax.random.normal(k, shape, jnp.float32)
    L, D = DEPTH, D_MODEL
    return {
        "x": nrm(ks[0], (BATCH, SEQ, D), 1.0),
        "c": nrm(ks[1], (BATCH, D), 1.0),
        "w_ada": nrm(ks[2], (L, D, N_MOD * D), 0.5 * D ** -0.5),
        "b_ada": nrm(ks[3], (L, N_MOD * D), 0.01),
        "norm_ffn1": gain(ks[4], (L, D)),
        "w_ffn1_in": nrm(ks[5], (L, D, 2 * D_FF), D ** -0.5),
        "w_ffn1_out": nrm(ks[6], (L, D_FF, D), D_FF ** -0.5),
        "norm_mix": gain(ks[7], (L, D)),
        "w_in": nrm(ks[8], (L, D, IN_WIDTH), D ** -0.5),
        "lambda_q1": nrm(ks[9], (L, DIFF_DIM), 0.1),
        "lambda_k1": nrm(ks[10], (L, DIFF_DIM), 0.1),
        "lambda_q2": nrm(ks[11], (L, DIFF_DIM), 0.1),
        "lambda_k2": nrm(ks[12], (L, DIFF_DIM), 0.1),
        "subln_gain": gain(ks[13], (L, 2 * DIFF_DIM)),
        "group_scale": gain(ks[14], (L, MIX_WIDTH)),
        "w_out": nrm(ks[15], (L, MIX_WIDTH, D), MIX_WIDTH ** -0.5),
        "norm_ffn2": gain(ks[16], (L, D)),
        "w_ffn2_in": nrm(ks[17], (L, D, 2 * D_FF), D ** -0.5),
        "w_ffn2_out": nrm(ks[18], (L, D_FF, D), D_FF ** -0.5),
        "rel_bias": nrm(ks[19], (N_BUCKETS, DIFF_HEADS), 0.5),
        "norm_final": gain(ks[20], (D,)),
    }


def reference(x, c, w_ada, b_ada, norm_ffn1, w_ffn1_in, w_ffn1_out, norm_mix, w_in,
              lambda_q1, lambda_k1, lambda_q2, lambda_k2, subln_gain, group_scale, w_out,
              norm_ffn2, w_ffn2_in, w_ffn2_out, rel_bias, norm_final):
    B = x.shape[0]
    for l in range(DEPTH):
        mod = (jax.nn.silu(c) @ w_ada[l] + b_ada[l]).reshape(B, N_MOD, D_MODEL)
        sh1, sc1, gt1 = mod[:, 0], mod[:, 1], mod[:, 2]
        shm, scm, gtm = mod[:, 3], mod[:, 4], mod[:, 5]
        sh2, sc2, gt2 = mod[:, 6], mod[:, 7], mod[:, 8]

        h = modulate(rmsnorm(x, norm_ffn1[l]), sh1, sc1)
        x = x + 0.5 * gt1[:, None, :] * swiglu(h, w_ffn1_in[l], w_ffn1_out[l])

        lambda_init = 0.8 - 0.6 * math.exp(-0.3 * l)
        lq1 = lambda_q1[l].astype(jnp.float32); lk1 = lambda_k1[l].astype(jnp.float32)
        lq2 = lambda_q2[l].astype(jnp.float32); lk2 = lambda_k2[l].astype(jnp.float32)
        lam = jnp.exp(jnp.sum(lq1 * lk1)) - jnp.exp(jnp.sum(lq2 * lk2)) + lambda_init
        h = modulate(rmsnorm(x, norm_mix[l]), shm, scm)
        x = x + gtm[:, None, :] * token_mixer(h, w_in[l], lam, lambda_init, subln_gain[l],
                                              group_scale[l], w_out[l], rel_bias)

        h = modulate(rmsnorm(x, norm_ffn2[l]), sh2, sc2)
        x = x + 0.5 * gt2[:, None, :] * swiglu(h, w_ffn2_in[l], w_ffn2_out[l])
    return rmsnorm(x, norm_final)
```

```python
import functools
import math

import jax
import jax.numpy as jnp
import numpy as np
from jax import lax
from jax.experimental import pallas as pl
from jax.experimental.pallas import tpu as pltpu

D_MODEL = 1024
D_FF = 2816
EPS = 1e-6
RET_DIM = 64
RET_WIDTH = 512
RET_HEADS = 8
DIFF_DIM = 64
DIFF_WIDTH = 512
DIFF_HEADS = 4
IN_WIDTH = 4 * RET_WIDTH + 3 * DIFF_WIDTH
N_BUCKETS = 32
MAX_DIST = 128
CHUNK = 128
ROPE_BASE = 10000.0
N_MOD = 9
NEG_INF = -1e30
LAMBDA_INIT = 0.8 - 0.6 * math.exp(-0.3 * 0)

V7X_LANES = 128
V7X_MXU_DIM = 256
V7X_VMEM_BYTES = 64 * 1024 * 1024

FFN_TM = 512
FFN_TF = V7X_MXU_DIM
PROJ_TM = 512
ATT_T = 256
SEG = 512

BF16 = jnp.bfloat16
F32 = jnp.float32


def _vmem_limit(nbytes):
    return int(min(nbytes + (12 << 20), V7X_VMEM_BYTES - (4 << 20)))


def _dot(a, b):
    return jnp.dot(a, b, preferred_element_type=F32)


def _dot_nt(a, b):
    return lax.dot_general(a, b, (((1,), (1,)), ((), ())), preferred_element_type=F32)


def _silu(x):
    return x * (1.0 / (1.0 + jnp.exp(-x)))


def _rms_mod(x, g, shift, scale):
    y = x * lax.rsqrt(jnp.mean(x * x, axis=-1, keepdims=True) + EPS)
    return (y * g) * (1.0 + scale) + shift


def _mod_kernel(c_ref, w_ref, b_ref, o_ref):
    c = c_ref[...]
    o_ref[...] = _dot(_silu(c).astype(BF16), w_ref[...].astype(BF16)) + b_ref[...]


def _mod_call(c, w_ada, b_ada):
    B = c.shape[0]
    n = N_MOD * D_MODEL
    tn = D_MODEL
    return pl.pallas_call(
        _mod_kernel,
        grid=(n // tn,),
        in_specs=[
            pl.BlockSpec((B, D_MODEL), lambda j: (0, 0)),
            pl.BlockSpec((D_MODEL, tn), lambda j: (0, j)),
            pl.BlockSpec((1, tn), lambda j: (0, j)),
        ],
        out_specs=pl.BlockSpec((B, tn), lambda j: (0, j)),
        out_shape=jax.ShapeDtypeStruct((B, n), F32),
        compiler_params=pltpu.CompilerParams(
            dimension_semantics=("parallel",),
            vmem_limit_bytes=_vmem_limit(2 * D_MODEL * tn * 4)),
        name="mod",
    )(c, w_ada, b_ada.reshape(1, n))


def _ffn_kernel(x_ref, mod_ref, g_ref, win_ref, wout_ref, gf_ref, o_ref, h_ref, act_ref, *, mod_row, final):
    x = x_ref[0]
    shift = mod_ref[0, mod_row:mod_row + 1, :]
    scale = mod_ref[0, mod_row + 1:mod_row + 2, :]
    gate = mod_ref[0, mod_row + 2:mod_row + 3, :]
    h_ref[...] = _rms_mod(x, g_ref[...], shift, scale).astype(BF16)
    for j in range(D_FF // FFN_TF):
        lo = j * FFN_TF
        g = _dot(h_ref[...], win_ref[:, lo:lo + FFN_TF])
        u = _dot(h_ref[...], win_ref[:, D_FF + lo:D_FF + lo + FFN_TF])
        act_ref[:, lo:lo + FFN_TF] = (_silu(g) * u).astype(BF16)
    y = x + (0.5 * gate) * _dot(act_ref[...], wout_ref[...])
    if final:
        y = (y * lax.rsqrt(jnp.mean(y * y, axis=-1, keepdims=True) + EPS)) * gf_ref[...]
    o_ref[0] = y


def _ffn_call(x, mod, norm_g, w_in, w_out, norm_final, *, mod_row, final):
    B, S, D = x.shape
    tm = FFN_TM
    const = lambda b, i: (0, 0)
    nbytes = (4 * tm * D * 4 + 2 * (D * 2 * D_FF + D_FF * D) * 2 + tm * D * 2 + tm * D_FF * 2)
    return pl.pallas_call(
        functools.partial(_ffn_kernel, mod_row=mod_row, final=final),
        grid=(B, S // tm),
        in_specs=[
            pl.BlockSpec((1, tm, D), lambda b, i: (b, i, 0)),
            pl.BlockSpec((1, N_MOD, D), lambda b, i: (b, 0, 0)),
            pl.BlockSpec((1, D), const),
            pl.BlockSpec((D, 2 * D_FF), const),
            pl.BlockSpec((D_FF, D), const),
            pl.BlockSpec((1, D), const),
        ],
        out_specs=pl.BlockSpec((1, tm, D), lambda b, i: (b, i, 0)),
        out_shape=jax.ShapeDtypeStruct((B, S, D), F32),
        scratch_shapes=[pltpu.VMEM((tm, D), BF16), pltpu.VMEM((tm, D_FF), BF16)],
        compiler_params=pltpu.CompilerParams(
            dimension_semantics=("parallel", "parallel"),
            vmem_limit_bytes=_vmem_limit(nbytes)),
        name="ffn_final" if final else "ffn",
    )(x, mod, norm_g.reshape(1, D), w_in, w_out, norm_final.reshape(1, D))


def _inproj_kernel(x_ref, mod_ref, g_ref, w_ref, cos_ref, sin_ref, o_ref, h_ref):
    x = x_ref[0]
    h_ref[...] = _rms_mod(x, g_ref[...], mod_ref[0, 3:4, :], mod_ref[0, 4:5, :]).astype(BF16)
    lane = lax.broadcasted_iota(jnp.int32, (1, V7X_LANES), 1)
    first_half = (lane % RET_DIM) < (RET_DIM // 2)
    cos = cos_ref[...]
    sin = sin_ref[...]
    for seg in range(IN_WIDTH // SEG):
        p = _dot(h_ref[...], w_ref[:, seg * SEG:(seg + 1) * SEG])
        if seg in (0, 1):
            post = RET_DIM ** -0.5 if seg == 1 else 1.0
            for c in range(SEG // V7X_LANES):
                v = p[:, c * V7X_LANES:(c + 1) * V7X_LANES]
                rot = jnp.where(first_half,
                                pltpu.roll(v, V7X_LANES - RET_DIM // 2, axis=1),
                                pltpu.roll(v, RET_DIM // 2, axis=1))
                r = v * cos + rot * sin
                if post != 1.0:
                    r = r * post
                o_ref[0, :, seg * SEG + c * V7X_LANES:seg * SEG + (c + 1) * V7X_LANES] = r.astype(BF16)
        elif seg == 3:
            o_ref[0, :, seg * SEG:(seg + 1) * SEG] = _silu(p).astype(BF16)
        elif seg == 4:
            o_ref[0, :, seg * SEG:(seg + 1) * SEG] = (p * (DIFF_DIM ** -0.5)).astype(BF16)
        else:
            o_ref[0, :, seg * SEG:(seg + 1) * SEG] = p.astype(BF16)


def _inproj_call(x, mod, norm_g, w_in, cos_tab, sin_tab):
    B, S, D = x.shape
    tm = PROJ_TM
    const = lambda b, i: (0, 0)
    nbytes = 2 * tm * D * 4 + 2 * D * IN_WIDTH * 2 + 2 * tm * IN_WIDTH * 2 + tm * D * 2 + 4 * tm * V7X_LANES * 4
    return pl.pallas_call(
        _inproj_kernel,
        grid=(B, S // tm),
        in_specs=[
            pl.BlockSpec((1, tm, D), lambda b, i: (b, i, 0)),
            pl.BlockSpec((1, N_MOD, D), lambda b, i: (b, 0, 0)),
            pl.BlockSpec((1, D), const),
            pl.BlockSpec((D, IN_WIDTH), const),
            pl.BlockSpec((tm, V7X_LANES), lambda b, i: (i, 0)),
            pl.BlockSpec((tm, V7X_LANES), lambda b, i: (i, 0)),
        ],
        out_specs=pl.BlockSpec((1, tm, IN_WIDTH), lambda b, i: (b, i, 0)),
        out_shape=jax.ShapeDtypeStruct((B, S, IN_WIDTH), BF16),
        scratch_shapes=[pltpu.VMEM((tm, D), BF16)],
        compiler_params=pltpu.CompilerParams(
            dimension_semantics=("parallel", "parallel"),
            vmem_limit_bytes=_vmem_limit(nbytes)),
        name="inproj",
    )(x, mod, norm_g.reshape(1, D), w_in, cos_tab, sin_tab)


def _bias_kernel(rb_ref, bk_ref, o_ref):
    h = pl.program_id(0)
    far = rb_ref[N_BUCKETS - 1, h]
    for t in range(2):
        bk = bk_ref[t]
        tile = jnp.zeros(bk.shape, F32)
        for b in range(N_BUCKETS - 1):
            tile = jnp.where(bk == b, rb_ref[b, h] - far, tile)
        o_ref[0, t] = jnp.where(bk < 0, NEG_INF, tile)


def _bias_call(rel_bias, buckets):
    T = ATT_T
    return pl.pallas_call(
        _bias_kernel,
        grid=(DIFF_HEADS,),
        in_specs=[
            pl.BlockSpec(memory_space=pltpu.SMEM),
            pl.BlockSpec((2, T, T), lambda h: (0, 0, 0)),
        ],
        out_specs=pl.BlockSpec((1, 2, T, T), lambda h: (h, 0, 0, 0)),
        out_shape=jax.ShapeDtypeStruct((DIFF_HEADS, 2, T, T), F32),
        compiler_params=pltpu.CompilerParams(dimension_semantics=("parallel",)),
        name="t5bias",
    )(rel_bias, buckets)


def _ret_kernel(q_ref, k_ref, v_ref, gate_ref, dec_ref, zeta_ref, xi_ref, gc_ref, o_ref, r_ref):
    C = CHUNK
    n_chunks = q_ref.shape[1] // C
    lane = lax.broadcasted_iota(jnp.int32, (1, V7X_LANES), 1)
    m_a = jnp.where(lane < RET_DIM, 1.0, 0.0).astype(BF16)
    m_b = jnp.where(lane >= RET_DIM, 1.0, 0.0).astype(BF16)
    row = lax.broadcasted_iota(jnp.int32, (V7X_LANES, V7X_LANES), 0)
    col = lax.broadcasted_iota(jnp.int32, (V7X_LANES, V7X_LANES), 1)
    same_head = jnp.where((row < RET_DIM) == (col < RET_DIM), 1.0, 0.0)
    seg_mean = (same_head * (1.0 / RET_DIM)).astype(BF16)
    gc = gc_ref[0]

    def state_step(n, r):
        r_ref[n] = r.astype(BF16)
        sl = pl.ds(pl.multiple_of(n * C, C), C)
        kz = (k_ref[0, sl, :].astype(F32) * zeta_ref[0]).T.astype(BF16)
        kv = _dot(kz, v_ref[0, sl, :])
        return r * gc + kv * same_head

    lax.fori_loop(0, n_chunks, state_step, jnp.zeros((V7X_LANES, V7X_LANES), F32))

    def out_step(n, carry):
        sl = pl.ds(pl.multiple_of(n * C, C), C)
        q = q_ref[0, sl, :]
        k = k_ref[0, sl, :]
        v = v_ref[0, sl, :]
        kk = jnp.concatenate([k * m_a, k * m_b], axis=0)
        vv = jnp.concatenate([v * m_a, v * m_b], axis=0)
        s = _dot_nt(q, kk) * dec_ref[0]
        y = _dot(s.astype(BF16), vv) + _dot(q, r_ref[n]) * xi_ref[0]
        mu = _dot(y.astype(BF16), seg_mean)
        d = y - mu
        var = _dot((d * d).astype(BF16), seg_mean)
        yn = d * lax.rsqrt(var + EPS)
        o_ref[0, sl, :] = (yn * gate_ref[0, sl, :].astype(F32)).astype(BF16)
        return carry

    lax.fori_loop(0, n_chunks, out_step, 0)


def _ret_call(proj, dec, zeta, xi, gc):
    B, S, _ = proj.shape
    n_pairs = RET_HEADS // 2
    blk = lambda off: pl.BlockSpec((1, S, V7X_LANES), lambda b, p: (b, 0, off + p))
    pair = lambda shape: pl.BlockSpec((1,) + shape, lambda b, p: (p, 0, 0))
    return pl.pallas_call(
        _ret_kernel,
        grid=(B, n_pairs),
        in_specs=[
            blk(0), blk(n_pairs), blk(2 * n_pairs), blk(3 * n_pairs),
            pair((CHUNK, 2 * CHUNK)), pair((CHUNK, V7X_LANES)), pair((CHUNK, V7X_LANES)),
            pair((V7X_LANES, V7X_LANES)),
        ],
        out_specs=pl.BlockSpec((1, S, V7X_LANES), lambda b, p: (b, 0, p)),
        out_shape=jax.ShapeDtypeStruct((B, S, RET_WIDTH), BF16),
        scratch_shapes=[pltpu.VMEM((S // CHUNK, V7X_LANES, V7X_LANES), BF16)],
        compiler_params=pltpu.CompilerParams(dimension_semantics=("parallel", "parallel")),
        name="retention",
    )(proj, proj, proj, proj, dec, zeta, xi, gc)


def _attn_kernel(q_ref, k_ref, v_ref, bias_ref, lq1_ref, lk1_ref, lq2_ref, lk2_ref, g_ref, o_ref,
                 vt_ref, m_ref, l_ref, acc_ref):
    T = ATT_T
    n_blocks = q_ref.shape[1] // T
    lane = lax.broadcasted_iota(jnp.int32, (1, V7X_LANES), 1)
    masks = (jnp.where(lane < DIFF_DIM, 1.0, 0.0).astype(BF16),
             jnp.where(lane >= DIFF_DIM, 1.0, 0.0).astype(BF16))
    lam = (jnp.exp(jnp.sum(lq1_ref[...] * lk1_ref[...], axis=-1, keepdims=True))
           - jnp.exp(jnp.sum(lq2_ref[...] * lk2_ref[...], axis=-1, keepdims=True)) + LAMBDA_INIT)

    def transpose_v(j, carry):
        sl = pl.ds(pl.multiple_of(j * T, T), T)
        vt_ref[j] = v_ref[0, sl, :].astype(F32).T.astype(BF16)
        return carry

    lax.fori_loop(0, n_blocks, transpose_v, 0)

    def tile_update(qm, kj, bias):
        sl = pl.ds(pl.multiple_of(kj * T, T), T)
        k = k_ref[0, sl, :]
        vt = vt_ref[kj]
        for m in range(2):
            s = _dot_nt(k, qm[m])
            if bias is not None:
                s = s + bias
            m_old = m_ref[m]
            m_new = jnp.maximum(m_old, jnp.max(s, axis=0, keepdims=True))
            p = jnp.exp(s - m_new)
            alpha = jnp.exp(m_old - m_new)
            l_ref[m] = alpha * l_ref[m] + jnp.sum(p, axis=0, keepdims=True)
            acc_ref[m] = alpha * acc_ref[m] + _dot(vt, p.astype(BF16))
            m_ref[m] = m_new

    def q_block(qi, carry):
        sl = pl.ds(pl.multiple_of(qi * T, T), T)
        q = q_ref[0, sl, :]
        qm = (q * masks[0], q * masks[1])
        m_ref[...] = jnp.full(m_ref.shape, NEG_INF, F32)
        l_ref[...] = jnp.zeros(l_ref.shape, F32)
        acc_ref[...] = jnp.zeros(acc_ref.shape, F32)

        def far_step(kj, c):
            tile_update(qm, kj, None)
            return c

        lax.fori_loop(0, qi - 1, far_step, 0)

        @pl.when(qi >= 1)
        def _():
            tile_update(qm, qi - 1, bias_ref[0, 1])

        tile_update(qm, qi, bias_ref[0, 0])

        o = acc_ref[0] / l_ref[0] - lam * (acc_ref[1] / l_ref[1])
        y = o * lax.rsqrt(jnp.mean(o * o, axis=0, keepdims=True) + EPS)
        y = (y * g_ref[...]) * (1.0 - LAMBDA_INIT)
        o_ref[0, sl, :] = y.T.astype(BF16)
        return carry

    lax.fori_loop(0, n_blocks, q_block, 0)


def _attn_call(proj, bias_tiles, lq1, lk1, lq2, lk2, subln_g):
    B, S, _ = proj.shape
    T = ATT_T
    q_off = 4 * RET_WIDTH // V7X_LANES
    blk = lambda off: pl.BlockSpec((1, S, V7X_LANES), lambda b, h: (b, 0, off + h))
    vec = pl.BlockSpec((1, DIFF_DIM), lambda b, h: (0, 0))
    return pl.pallas_call(
        _attn_kernel,
        grid=(B, DIFF_HEADS),
        in_specs=[
            blk(q_off), blk(q_off + DIFF_HEADS), blk(q_off + 2 * DIFF_HEADS),
            pl.BlockSpec((1, 2, T, T), lambda b, h: (h, 0, 0, 0)),
            vec, vec, vec, vec,
            pl.BlockSpec((2 * DIFF_DIM, 1), lambda b, h: (0, 0)),
        ],
        out_specs=pl.BlockSpec((1, S, V7X_LANES), lambda b, h: (b, 0, h)),
        out_shape=jax.ShapeDtypeStruct((B, S, DIFF_WIDTH), BF16),
        scratch_shapes=[
            pltpu.VMEM((S // T, 2 * DIFF_DIM, T), BF16),
            pltpu.VMEM((2, 1, T), F32),
            pltpu.VMEM((2, 1, T), F32),
            pltpu.VMEM((2, 2 * DIFF_DIM, T), F32),
        ],
        compiler_params=pltpu.CompilerParams(dimension_semantics=("parallel", "parallel")),
        name="diffattn",
    )(proj, proj, proj, bias_tiles, lq1, lk1, lq2, lk2, subln_g.reshape(2 * DIFF_DIM, 1))


def _outproj_kernel(x_ref, mod_ref, yr_ref, yd_ref, gs_ref, w_ref, o_ref):
    gs = gs_ref[...]
    ya = (yr_ref[0].astype(F32) * gs[:, :RET_WIDTH]).astype(BF16)
    yb = (yd_ref[0].astype(F32) * gs[:, RET_WIDTH:]).astype(BF16)
    mix = _dot(ya, w_ref[:RET_WIDTH, :]) + _dot(yb, w_ref[RET_WIDTH:, :])
    o_ref[0] = x_ref[0] + mod_ref[0, 5:6, :] * mix


def _outproj_call(x, mod, y_ret, y_diff, group_scale, w_out):
    B, S, D = x.shape
    tm = PROJ_TM
    const = lambda b, i: (0, 0)
    nbytes = 4 * tm * D * 4 + 2 * D * D * 2 + 4 * tm * RET_WIDTH * 2
    return pl.pallas_call(
        _outproj_kernel,
        grid=(B, S // tm),
        in_specs=[
            pl.BlockSpec((1, tm, D), lambda b, i: (b, i, 0)),
            pl.BlockSpec((1, N_MOD, D), lambda b, i: (b, 0, 0)),
            pl.BlockSpec((1, tm, RET_WIDTH), lambda b, i: (b, i, 0)),
            pl.BlockSpec((1, tm, DIFF_WIDTH), lambda b, i: (b, i, 0)),
            pl.BlockSpec((1, D), const),
            pl.BlockSpec((D, D), const),
        ],
        out_specs=pl.BlockSpec((1, tm, D), lambda b, i: (b, i, 0)),
        out_shape=jax.ShapeDtypeStruct((B, S, D), F32),
        compiler_params=pltpu.CompilerParams(
            dimension_semantics=("parallel", "parallel"),
            vmem_limit_bytes=_vmem_limit(nbytes)),
        name="outproj",
    )(x, mod, y_ret, y_diff, group_scale.reshape(1, D), w_out)


def _rope_tables(S):
    half = RET_DIM // 2
    inv = ROPE_BASE ** (-jnp.arange(0, RET_DIM, 2, dtype=F32) / RET_DIM)
    ang = jnp.arange(S).astype(F32)[:, None] * inv[None, :]
    cos = jnp.cos(ang)
    sin = jnp.sin(ang)
    reps = V7X_LANES // RET_DIM
    cos_tab = jnp.tile(jnp.concatenate([cos, cos], axis=1), (1, reps))
    sin_tab = jnp.tile(jnp.concatenate([-sin, sin], axis=1), (1, reps))
    del half
    return cos_tab, sin_tab


def _retention_tables():
    C = CHUNK
    log_gamma = jnp.log1p(-(2.0 ** (-5.0 - jnp.arange(RET_HEADS, dtype=F32))))
    idx = jnp.arange(C, dtype=F32)
    dist = idx[:, None] - idx[None, :]
    decay = jnp.where(dist >= 0, jnp.exp(log_gamma[:, None, None] * jnp.maximum(dist, 0.0)[None]), 0.0)
    zeta = jnp.exp(log_gamma[:, None] * (C - 1 - idx)[None])
    xi = jnp.exp(log_gamma[:, None] * (idx + 1.0)[None])
    gamma_c = jnp.exp(log_gamma * C)
    n_pairs = RET_HEADS // 2
    dec = decay.reshape(n_pairs, 2, C, C).transpose(0, 2, 1, 3).reshape(n_pairs, C, 2 * C)
    lanes = lambda t: jnp.repeat(t.reshape(n_pairs, 2, C).transpose(0, 2, 1), RET_DIM, axis=2)
    g_rows = jnp.repeat(gamma_c.reshape(n_pairs, 2), RET_DIM, axis=1)
    blockdiag = (jnp.arange(V7X_LANES)[:, None] < RET_DIM) == (jnp.arange(V7X_LANES)[None, :] < RET_DIM)
    gc = jnp.where(blockdiag[None], g_rows[:, :, None], 0.0)
    return dec, lanes(zeta), lanes(xi), gc


def _t5_bucket(rel):
    n = jnp.maximum(rel, 0)
    max_exact = N_BUCKETS // 2
    nf = jnp.maximum(n, 1).astype(F32)
    large = max_exact + (jnp.log(nf / max_exact) / math.log(MAX_DIST / max_exact)
                         * (N_BUCKETS - max_exact)).astype(jnp.int32)
    large = jnp.minimum(large, N_BUCKETS - 1)
    return jnp.where(n < max_exact, n, large)


def _bucket_tiles():
    T = ATT_T
    kpos = jnp.arange(T)[:, None]
    qpos = jnp.arange(T)[None, :]
    rel0 = qpos - kpos
    rel1 = rel0 + T
    t0 = jnp.where(rel0 >= 0, _t5_bucket(rel0), -1)
    return jnp.stack([t0, _t5_bucket(rel1)]).astype(jnp.int32)


def kernel(x, c, w_ada, b_ada, norm_ffn1, w_ffn1_in, w_ffn1_out, norm_mix, w_in, lambda_q1, lambda_k1,
           lambda_q2, lambda_k2, subln_gain, group_scale, w_out, norm_ffn2, w_ffn2_in, w_ffn2_out, rel_bias,
           norm_final):
    B, S, D = x.shape
    l = 0
    mod = _mod_call(c, w_ada[l], b_ada[l]).reshape(B, N_MOD, D)

    x = _ffn_call(x, mod, norm_ffn1[l], w_ffn1_in[l].astype(BF16), w_ffn1_out[l].astype(BF16), norm_final,
                  mod_row=0, final=False)

    cos_tab, sin_tab = _rope_tables(S)
    proj = _inproj_call(x, mod, norm_mix[l], w_in[l].astype(BF16), cos_tab, sin_tab)
    y_ret = _ret_call(proj, *_retention_tables())
    bias_tiles = _bias_call(rel_bias, _bucket_tiles())
    y_diff = _attn_call(proj, bias_tiles, lambda_q1[l][None], lambda_k1[l][None], lambda_q2[l][None],
                        lambda_k2[l][None], subln_gain[l])
    x = _outproj_call(x, mod, y_ret, y_diff, group_scale[l], w_out[l].astype(BF16))

    return _ffn_call(x, mod, norm_ffn2[l], w_ffn2_in[l].astype(BF16), w_ffn2_out[l].astype(BF16), norm_final,
                     mod_row=6, final=True)
```

```python
import functools
import math

import jax
import jax.numpy as jnp
import numpy as np
from jax import lax
from jax.experimental import pallas as pl
from jax.experimental.pallas import tpu as pltpu

D_MODEL = 1024
D_FF = 2816
EPS = 1e-6
RET_DIM = 64
RET_WIDTH = 512
RET_HEADS = 8
DIFF_DIM = 64
DIFF_WIDTH = 512
DIFF_HEADS = 4
IN_WIDTH = 4 * RET_WIDTH + 3 * DIFF_WIDTH
N_BUCKETS = 32
MAX_DIST = 128
CHUNK = 128
ROPE_BASE = 10000.0
N_MOD = 9
NEG_INF = -1e30
LAMBDA_INIT = 0.8 - 0.6 * math.exp(-0.3 * 0)

V7X_LANES = 128
V7X_MXU_DIM = 256
V7X_VMEM_BYTES = 64 * 1024 * 1024

FFN_TM = 512
FFN_TF = V7X_MXU_DIM
PROJ_TM = 512
ATT_TQ = 512
ATT_TK = 1024
RET_UNROLL = 8
LN_ROWS = 1024
LOG2E = math.log2(math.e)
SEG = 512

BF16 = jnp.bfloat16
F32 = jnp.float32


def _vmem_limit(nbytes):
    return int(min(nbytes + (12 << 20), V7X_VMEM_BYTES - (4 << 20)))


def _dot(a, b):
    return jnp.dot(a, b, preferred_element_type=F32)


def _dot_nt(a, b):
    return lax.dot_general(a, b, (((1,), (1,)), ((), ())), preferred_element_type=F32)


def _silu(x):
    return x * (1.0 / (1.0 + jnp.exp(-x)))


def _rms_mod(x, g, shift, scale):
    y = x * lax.rsqrt(jnp.mean(x * x, axis=-1, keepdims=True) + EPS)
    return (y * g) * (1.0 + scale) + shift


def _mod_kernel(c_ref, w_ref, b_ref, o_ref):
    c = c_ref[...]
    o_ref[...] = _dot(_silu(c).astype(BF16), w_ref[...].astype(BF16)) + b_ref[...]


def _mod_call(c, w_ada, b_ada):
    B = c.shape[0]
    n = N_MOD * D_MODEL
    tn = D_MODEL
    return pl.pallas_call(
        _mod_kernel,
        grid=(n // tn,),
        in_specs=[
            pl.BlockSpec((B, D_MODEL), lambda j: (0, 0)),
            pl.BlockSpec((D_MODEL, tn), lambda j: (0, j)),
            pl.BlockSpec((1, tn), lambda j: (0, j)),
        ],
        out_specs=pl.BlockSpec((B, tn), lambda j: (0, j)),
        out_shape=jax.ShapeDtypeStruct((B, n), F32),
        compiler_params=pltpu.CompilerParams(
            dimension_semantics=("parallel",),
            vmem_limit_bytes=_vmem_limit(2 * D_MODEL * tn * 4)),
        name="mod",
    )(c, w_ada, b_ada.reshape(1, n))


def _ffn_kernel(x_ref, mod_ref, g_ref, win_ref, wout_ref, gf_ref, o_ref, h_ref, act_ref, *, mod_row, final):
    x = x_ref[0]
    shift = mod_ref[0, mod_row:mod_row + 1, :]
    scale = mod_ref[0, mod_row + 1:mod_row + 2, :]
    gate = mod_ref[0, mod_row + 2:mod_row + 3, :]
    h_ref[...] = _rms_mod(x, g_ref[...], shift, scale).astype(BF16)
    for j in range(D_FF // FFN_TF):
        lo = j * FFN_TF
        g = _dot(h_ref[...], win_ref[:, lo:lo + FFN_TF])
        u = _dot(h_ref[...], win_ref[:, D_FF + lo:D_FF + lo + FFN_TF])
        act_ref[:, lo:lo + FFN_TF] = (_silu(g) * u).astype(BF16)
    y = x + (0.5 * gate) * _dot(act_ref[...], wout_ref[...])
    if final:
        y = (y * lax.rsqrt(jnp.mean(y * y, axis=-1, keepdims=True) + EPS)) * gf_ref[...]
    o_ref[0] = y


def _ffn_call(x, mod, norm_g, w_in, w_out, norm_final, *, mod_row, final):
    B, S, D = x.shape
    tm = FFN_TM
    const = lambda b, i: (0, 0)
    nbytes = (4 * tm * D * 4 + 2 * (D * 2 * D_FF + D_FF * D) * 2 + tm * D * 2 + tm * D_FF * 2)
    return pl.pallas_call(
        functools.partial(_ffn_kernel, mod_row=mod_row, final=final),
        grid=(B, S // tm),
        in_specs=[
            pl.BlockSpec((1, tm, D), lambda b, i: (b, i, 0)),
            pl.BlockSpec((1, N_MOD, D), lambda b, i: (b, 0, 0)),
            pl.BlockSpec((1, D), const),
            pl.BlockSpec((D, 2 * D_FF), const),
            pl.BlockSpec((D_FF, D), const),
            pl.BlockSpec((1, D), const),
        ],
        out_specs=pl.BlockSpec((1, tm, D), lambda b, i: (b, i, 0)),
        out_shape=jax.ShapeDtypeStruct((B, S, D), F32),
        scratch_shapes=[pltpu.VMEM((tm, D), BF16), pltpu.VMEM((tm, D_FF), BF16)],
        compiler_params=pltpu.CompilerParams(
            dimension_semantics=("parallel", "parallel"),
            vmem_limit_bytes=_vmem_limit(nbytes)),
        name="ffn_final" if final else "ffn",
    )(x, mod, norm_g.reshape(1, D), w_in, w_out, norm_final.reshape(1, D))


def _inproj_kernel(x_ref, mod_ref, g_ref, w_ref, cos_ref, sin_ref, o_ref, h_ref):
    x = x_ref[0]
    h_ref[...] = _rms_mod(x, g_ref[...], mod_ref[0, 3:4, :], mod_ref[0, 4:5, :]).astype(BF16)
    lane = lax.broadcasted_iota(jnp.int32, (1, V7X_LANES), 1)
    first_half = (lane % RET_DIM) < (RET_DIM // 2)
    cos = cos_ref[...]
    sin = sin_ref[...]
    for seg in range(IN_WIDTH // SEG):
        p = _dot(h_ref[...], w_ref[:, seg * SEG:(seg + 1) * SEG])
        if seg in (0, 1):
            post = RET_DIM ** -0.5 if seg == 1 else 1.0
            for c in range(SEG // V7X_LANES):
                v = p[:, c * V7X_LANES:(c + 1) * V7X_LANES]
                rot = jnp.where(first_half,
                                pltpu.roll(v, V7X_LANES - RET_DIM // 2, axis=1),
                                pltpu.roll(v, RET_DIM // 2, axis=1))
                r = v * cos + rot * sin
                if post != 1.0:
                    r = r * post
                o_ref[0, :, seg * SEG + c * V7X_LANES:seg * SEG + (c + 1) * V7X_LANES] = r.astype(BF16)
        elif seg == 3:
            o_ref[0, :, seg * SEG:(seg + 1) * SEG] = _silu(p).astype(BF16)
        elif seg == 4:
            o_ref[0, :, seg * SEG:(seg + 1) * SEG] = (p * (LOG2E * DIFF_DIM ** -0.5)).astype(BF16)
        else:
            o_ref[0, :, seg * SEG:(seg + 1) * SEG] = p.astype(BF16)


def _inproj_call(x, mod, norm_g, w_in, cos_tab, sin_tab):
    B, S, D = x.shape
    tm = PROJ_TM
    const = lambda b, i: (0, 0)
    nbytes = 2 * tm * D * 4 + 2 * D * IN_WIDTH * 2 + 2 * tm * IN_WIDTH * 2 + tm * D * 2 + 4 * tm * V7X_LANES * 4
    return pl.pallas_call(
        _inproj_kernel,
        grid=(B, S // tm),
        in_specs=[
            pl.BlockSpec((1, tm, D), lambda b, i: (b, i, 0)),
            pl.BlockSpec((1, N_MOD, D), lambda b, i: (b, 0, 0)),
            pl.BlockSpec((1, D), const),
            pl.BlockSpec((D, IN_WIDTH), const),
            pl.BlockSpec((tm, V7X_LANES), lambda b, i: (i, 0)),
            pl.BlockSpec((tm, V7X_LANES), lambda b, i: (i, 0)),
        ],
        out_specs=pl.BlockSpec((1, tm, IN_WIDTH), lambda b, i: (b, i, 0)),
        out_shape=jax.ShapeDtypeStruct((B, S, IN_WIDTH), BF16),
        scratch_shapes=[pltpu.VMEM((tm, D), BF16)],
        compiler_params=pltpu.CompilerParams(
            dimension_semantics=("parallel", "parallel"),
            vmem_limit_bytes=_vmem_limit(nbytes)),
        name="inproj",
    )(x, mod, norm_g.reshape(1, D), w_in, cos_tab, sin_tab)


def _bias_kernel(rb_ref, bk_ref, o_ref):
    T = ATT_TQ
    R = ATT_TK // ATT_TQ
    h = pl.program_id(0)
    far = rb_ref[N_BUCKETS - 1, h]
    near = []
    for t in range(2):
        bk = bk_ref[t]
        tile = jnp.zeros(bk.shape, F32)
        for b in range(N_BUCKETS - 1):
            tile = jnp.where(bk == b, (rb_ref[b, h] - far) * LOG2E, tile)
        near.append(jnp.where(bk < 0, NEG_INF, tile))
    zeros = jnp.zeros((T, T), F32)
    masked = jnp.full((T, T), NEG_INF, F32)
    for t in range(R + 1):
        for a in range(R):
            d = t - a if t < R else R - a
            blk = masked if d < 0 else near[d] if d < 2 else zeros
            o_ref[0, t, a * T:(a + 1) * T, :] = blk


def _bias_call(rel_bias, buckets):
    T = ATT_TQ
    R = ATT_TK // ATT_TQ
    return pl.pallas_call(
        _bias_kernel,
        grid=(DIFF_HEADS,),
        in_specs=[
            pl.BlockSpec(memory_space=pltpu.SMEM),
            pl.BlockSpec((2, T, T), lambda h: (0, 0, 0)),
        ],
        out_specs=pl.BlockSpec((1, R + 1, ATT_TK, T), lambda h: (h, 0, 0, 0)),
        out_shape=jax.ShapeDtypeStruct((DIFF_HEADS, R + 1, ATT_TK, T), F32),
        compiler_params=pltpu.CompilerParams(
            dimension_semantics=("parallel",),
            vmem_limit_bytes=_vmem_limit(2 * (R + 1) * ATT_TK * T * 4)),
        name="t5bias",
    )(rel_bias, buckets)


def _ret_kernel(q_ref, k_ref, v_ref, gate_ref, dec_ref, zeta_ref, xi_ref, gc_ref, o_ref, kv_ref, r_ref, y_ref):
    C = CHUNK
    U = RET_UNROLL
    S = q_ref.shape[1]
    n_chunks = S // C
    lane = lax.broadcasted_iota(jnp.int32, (1, V7X_LANES), 1)
    m_a = jnp.where(lane < RET_DIM, 1.0, 0.0).astype(BF16)
    m_b = jnp.where(lane >= RET_DIM, 1.0, 0.0).astype(BF16)
    row = lax.broadcasted_iota(jnp.int32, (V7X_LANES, V7X_LANES), 0)
    col = lax.broadcasted_iota(jnp.int32, (V7X_LANES, V7X_LANES), 1)
    same_head = jnp.where((row < RET_DIM) == (col < RET_DIM), 1.0, 0.0)
    seg_mean = (same_head * (1.0 / RET_DIM)).astype(BF16)
    gc = gc_ref[0]

    def kv_group(g, carry):
        for u in range(U):
            n = g * U + u
            sl = pl.ds(pl.multiple_of(n * C, C), C)
            kz = (k_ref[0, sl, :].astype(F32) * zeta_ref[0]).T.astype(BF16)
            kv_ref[n] = _dot(kz, v_ref[0, sl, :]) * same_head
        return carry

    lax.fori_loop(0, n_chunks // U, kv_group, 0)

    def scan_step(n, r):
        r_ref[n] = r.astype(BF16)
        return r * gc + kv_ref[n]

    lax.fori_loop(0, n_chunks, scan_step, jnp.zeros((V7X_LANES, V7X_LANES), F32))

    def out_group(g, carry):
        for u in range(U):
            n = g * U + u
            sl = pl.ds(pl.multiple_of(n * C, C), C)
            q = q_ref[0, sl, :]
            k = k_ref[0, sl, :]
            v = v_ref[0, sl, :]
            kk = jnp.concatenate([k * m_a, k * m_b], axis=0)
            vv = jnp.concatenate([v * m_a, v * m_b], axis=0)
            s = _dot_nt(q, kk) * dec_ref[0]
            y_ref[sl, :] = _dot(s.astype(BF16), vv) + _dot(q, r_ref[n]) * xi_ref[0]
        return carry

    lax.fori_loop(0, n_chunks // U, out_group, 0)

    def ln_block(i, carry):
        sl = pl.ds(pl.multiple_of(i * LN_ROWS, LN_ROWS), LN_ROWS)
        y = y_ref[sl, :]
        d = y - _dot(y.astype(BF16), seg_mean)
        var = _dot((d * d).astype(BF16), seg_mean)
        o_ref[0, sl, :] = ((d * lax.rsqrt(var + EPS)) * gate_ref[0, sl, :].astype(F32)).astype(BF16)
        return carry

    lax.fori_loop(0, S // LN_ROWS, ln_block, 0)


def _ret_call(proj, dec, zeta, xi, gc):
    B, S, _ = proj.shape
    n_pairs = RET_HEADS // 2
    blk = lambda off: pl.BlockSpec((1, S, V7X_LANES), lambda b, p: (b, 0, off + p))
    pair = lambda shape: pl.BlockSpec((1,) + shape, lambda b, p: (p, 0, 0))
    return pl.pallas_call(
        _ret_kernel,
        grid=(B, n_pairs),
        in_specs=[
            blk(0), blk(n_pairs), blk(2 * n_pairs), blk(3 * n_pairs),
            pair((CHUNK, 2 * CHUNK)), pair((CHUNK, V7X_LANES)), pair((CHUNK, V7X_LANES)),
            pair((V7X_LANES, V7X_LANES)),
        ],
        out_specs=pl.BlockSpec((1, S, V7X_LANES), lambda b, p: (b, 0, p)),
        out_shape=jax.ShapeDtypeStruct((B, S, RET_WIDTH), BF16),
        scratch_shapes=[
            pltpu.VMEM((S // CHUNK, V7X_LANES, V7X_LANES), F32),
            pltpu.VMEM((S // CHUNK, V7X_LANES, V7X_LANES), BF16),
            pltpu.VMEM((S, V7X_LANES), F32),
        ],
        compiler_params=pltpu.CompilerParams(dimension_semantics=("parallel", "parallel")),
        name="retention",
    )(proj, proj, proj, proj, dec, zeta, xi, gc)


def _attn_kernel(q_ref, k_ref, v_ref, bias_ref, lq1_ref, lk1_ref, lq2_ref, lk2_ref, g_ref, o_ref,
                 vt_ref, m_ref, l_ref, acc_ref):
    TQ, TK = ATT_TQ, ATT_TK
    R = TK // TQ
    S = q_ref.shape[1]
    lane = lax.broadcasted_iota(jnp.int32, (1, V7X_LANES), 1)
    masks = (jnp.where(lane < DIFF_DIM, 1.0, 0.0).astype(BF16),
             jnp.where(lane >= DIFF_DIM, 1.0, 0.0).astype(BF16))
    lam = (jnp.exp(jnp.sum(lq1_ref[...] * lk1_ref[...], axis=-1, keepdims=True))
           - jnp.exp(jnp.sum(lq2_ref[...] * lk2_ref[...], axis=-1, keepdims=True)) + LAMBDA_INIT)

    def transpose_v(c, carry):
        sl = pl.ds(pl.multiple_of(c * TK, TK), TK)
        vt_ref[c] = v_ref[0, sl, :].astype(F32).T.astype(BF16)
        return carry

    lax.fori_loop(0, S // TK, transpose_v, 0)

    def chunk_update(qm, c, bias):
        sl = pl.ds(pl.multiple_of(c * TK, TK), TK)
        k = k_ref[0, sl, :]
        vt = vt_ref[c]
        scores = [_dot_nt(k, qm[m]) for m in range(2)]
        for m in range(2):
            s = scores[m]
            if bias is not None:
                s = s + bias
            m_old = m_ref[m]
            m_new = jnp.maximum(m_old, jnp.max(s, axis=0, keepdims=True))
            p = jnp.exp2(s - m_new)
            alpha = jnp.exp2(m_old - m_new)
            l_ref[m] = alpha * l_ref[m] + jnp.sum(p, axis=0, keepdims=True)
            acc_ref[m] = alpha * acc_ref[m] + _dot(vt, p.astype(BF16))
            m_ref[m] = m_new

    def q_block(qi, carry):
        sl = pl.ds(pl.multiple_of(qi * TQ, TQ), TQ)
        q = q_ref[0, sl, :]
        qm = (q * masks[0], q * masks[1])
        m_ref[...] = jnp.full(m_ref.shape, NEG_INF, F32)
        l_ref[...] = jnp.zeros(l_ref.shape, F32)
        acc_ref[...] = jnp.zeros(acc_ref.shape, F32)
        c_diag = qi // R
        first = (qi % R) == 0

        def far_step(c, carry_):
            chunk_update(qm, c, None)
            return carry_

        lax.fori_loop(0, jnp.where(first, c_diag - 1, c_diag), far_step, 0)

        @pl.when(jnp.logical_and(first, c_diag >= 1))
        def _():
            chunk_update(qm, c_diag - 1, bias_ref[0, R])

        chunk_update(qm, c_diag, bias_ref[0, qi % R])

        o = acc_ref[0] / l_ref[0] - lam * (acc_ref[1] / l_ref[1])
        y = o * lax.rsqrt(jnp.mean(o * o, axis=0, keepdims=True) + EPS)
        y = (y * g_ref[...]) * (1.0 - LAMBDA_INIT)
        o_ref[0, sl, :] = y.T.astype(BF16)
        return carry

    lax.fori_loop(0, S // TQ, q_block, 0)


def _attn_call(proj, bias_tiles, lq1, lk1, lq2, lk2, subln_g):
    B, S, _ = proj.shape
    TQ, TK = ATT_TQ, ATT_TK
    R = TK // TQ
    q_off = 4 * RET_WIDTH // V7X_LANES
    nbytes = (8 * S * V7X_LANES * 2 + 2 * (R + 1) * TK * TQ * 4 + S * V7X_LANES * 2
              + 2 * 2 * DIFF_DIM * TQ * 4 + 6 * TK * TQ * 4)
    blk = lambda off: pl.BlockSpec((1, S, V7X_LANES), lambda b, h: (b, 0, off + h))
    vec = pl.BlockSpec((1, DIFF_DIM), lambda b, h: (0, 0))
    return pl.pallas_call(
        _attn_kernel,
        grid=(B, DIFF_HEADS),
        in_specs=[
            blk(q_off), blk(q_off + DIFF_HEADS), blk(q_off + 2 * DIFF_HEADS),
            pl.BlockSpec((1, R + 1, TK, TQ), lambda b, h: (h, 0, 0, 0)),
            vec, vec, vec, vec,
            pl.BlockSpec((2 * DIFF_DIM, 1), lambda b, h: (0, 0)),
        ],
        out_specs=pl.BlockSpec((1, S, V7X_LANES), lambda b, h: (b, 0, h)),
        out_shape=jax.ShapeDtypeStruct((B, S, DIFF_WIDTH), BF16),
        scratch_shapes=[
            pltpu.VMEM((S // TK, 2 * DIFF_DIM, TK), BF16),
            pltpu.VMEM((2, 1, TQ), F32),
            pltpu.VMEM((2, 1, TQ), F32),
            pltpu.VMEM((2, 2 * DIFF_DIM, TQ), F32),
        ],
        compiler_params=pltpu.CompilerParams(
            dimension_semantics=("parallel", "parallel"),
            vmem_limit_bytes=_vmem_limit(nbytes)),
        name="diffattn",
    )(proj, proj, proj, bias_tiles, lq1, lk1, lq2, lk2, subln_g.reshape(2 * DIFF_DIM, 1))


def _outproj_kernel(x_ref, mod_ref, yr_ref, yd_ref, gs_ref, w_ref, o_ref):
    gs = gs_ref[...]
    ya = (yr_ref[0].astype(F32) * gs[:, :RET_WIDTH]).astype(BF16)
    yb = (yd_ref[0].astype(F32) * gs[:, RET_WIDTH:]).astype(BF16)
    mix = _dot(ya, w_ref[:RET_WIDTH, :]) + _dot(yb, w_ref[RET_WIDTH:, :])
    o_ref[0] = x_ref[0] + mod_ref[0, 5:6, :] * mix


def _outproj_call(x, mod, y_ret, y_diff, group_scale, w_out):
    B, S, D = x.shape
    tm = PROJ_TM
    const = lambda b, i: (0, 0)
    nbytes = 4 * tm * D * 4 + 2 * D * D * 2 + 4 * tm * RET_WIDTH * 2
    return pl.pallas_call(
        _outproj_kernel,
        grid=(B, S // tm),
        in_specs=[
            pl.BlockSpec((1, tm, D), lambda b, i: (b, i, 0)),
            pl.BlockSpec((1, N_MOD, D), lambda b, i: (b, 0, 0)),
            pl.BlockSpec((1, tm, RET_WIDTH), lambda b, i: (b, i, 0)),
            pl.BlockSpec((1, tm, DIFF_WIDTH), lambda b, i: (b, i, 0)),
            pl.BlockSpec((1, D), const),
            pl.BlockSpec((D, D), const),
        ],
        out_specs=pl.BlockSpec((1, tm, D), lambda b, i: (b, i, 0)),
        out_shape=jax.ShapeDtypeStruct((B, S, D), F32),
        compiler_params=pltpu.CompilerParams(
            dimension_semantics=("parallel", "parallel"),
            vmem_limit_bytes=_vmem_limit(nbytes)),
        name="outproj",
    )(x, mod, y_ret, y_diff, group_scale.reshape(1, D), w_out)


def _rope_tables(S):
    half = RET_DIM // 2
    inv = ROPE_BASE ** (-jnp.arange(0, RET_DIM, 2, dtype=F32) / RET_DIM)
    ang = jnp.arange(S).astype(F32)[:, None] * inv[None, :]
    cos = jnp.cos(ang)
    sin = jnp.sin(ang)
    reps = V7X_LANES // RET_DIM
    cos_tab = jnp.tile(jnp.concatenate([cos, cos], axis=1), (1, reps))
    sin_tab = jnp.tile(jnp.concatenate([-sin, sin], axis=1), (1, reps))
    del half
    return cos_tab, sin_tab


def _retention_tables():
    C = CHUNK
    log_gamma = jnp.log1p(-(2.0 ** (-5.0 - jnp.arange(RET_HEADS, dtype=F32))))
    idx = jnp.arange(C, dtype=F32)
    dist = idx[:, None] - idx[None, :]
    decay = jnp.where(dist >= 0, jnp.exp(log_gamma[:, None, None] * jnp.maximum(dist, 0.0)[None]), 0.0)
    zeta = jnp.exp(log_gamma[:, None] * (C - 1 - idx)[None])
    xi = jnp.exp(log_gamma[:, None] * (idx + 1.0)[None])
    gamma_c = jnp.exp(log_gamma * C)
    n_pairs = RET_HEADS // 2
    dec = decay.reshape(n_pairs, 2, C, C).transpose(0, 2, 1, 3).reshape(n_pairs, C, 2 * C)
    lanes = lambda t: jnp.repeat(t.reshape(n_pairs, 2, C).transpose(0, 2, 1), RET_DIM, axis=2)
    g_rows = jnp.repeat(gamma_c.reshape(n_pairs, 2), RET_DIM, axis=1)
    blockdiag = (jnp.arange(V7X_LANES)[:, None] < RET_DIM) == (jnp.arange(V7X_LANES)[None, :] < RET_DIM)
    gc = jnp.where(blockdiag[None], g_rows[:, :, None], 0.0)
    return dec, lanes(zeta), lanes(xi), gc


def _t5_bucket(rel):
    n = jnp.maximum(rel, 0)
    max_exact = N_BUCKETS // 2
    nf = jnp.maximum(n, 1).astype(F32)
    large = max_exact + (jnp.log(nf / max_exact) / math.log(MAX_DIST / max_exact)
                         * (N_BUCKETS - max_exact)).astype(jnp.int32)
    large = jnp.minimum(large, N_BUCKETS - 1)
    return jnp.where(n < max_exact, n, large)


def _bucket_tiles():
    T = ATT_TQ
    kpos = jnp.arange(T)[:, None]
    qpos = jnp.arange(T)[None, :]
    rel0 = qpos - kpos
    rel1 = rel0 + T
    t0 = jnp.where(rel0 >= 0, _t5_bucket(rel0), -1)
    return jnp.stack([t0, _t5_bucket(rel1)]).astype(jnp.int32)


def kernel(x, c, w_ada, b_ada, norm_ffn1, w_ffn1_in, w_ffn1_out, norm_mix, w_in, lambda_q1, lambda_k1,
           lambda_q2, lambda_k2, subln_gain, group_scale, w_out, norm_ffn2, w_ffn2_in, w_ffn2_out, rel_bias,
           norm_final):
    B, S, D = x.shape
    l = 0
    mod = _mod_call(c, w_ada[l], b_ada[l]).reshape(B, N_MOD, D)

    x = _ffn_call(x, mod, norm_ffn1[l], w_ffn1_in[l].astype(BF16), w_ffn1_out[l].astype(BF16), norm_final,
                  mod_row=0, final=False)

    cos_tab, sin_tab = _rope_tables(S)
    proj = _inproj_call(x, mod, norm_mix[l], w_in[l].astype(BF16), cos_tab, sin_tab)
    y_ret = _ret_call(proj, *_retention_tables())
    bias_tiles = _bias_call(rel_bias, _bucket_tiles())
    y_diff = _attn_call(proj, bias_tiles, lambda_q1[l][None], lambda_k1[l][None], lambda_q2[l][None],
                        lambda_k2[l][None], subln_gain[l])
    x = _outproj_call(x, mod, y_ret, y_diff, group_scale[l], w_out[l].astype(BF16))

    return _ffn_call(x, mod, norm_ffn2[l], w_ffn2_in[l].astype(BF16), w_ffn2_out[l].astype(BF16), norm_final,
                     mod_row=6, final=True)
```

```python
import functools
import math

import jax
import jax.numpy as jnp
import numpy as np
from jax import lax
from jax.experimental import pallas as pl
from jax.experimental.pallas import tpu as pltpu

D_MODEL = 1024
D_FF = 2816
EPS = 1e-6
RET_DIM = 64
RET_WIDTH = 512
RET_HEADS = 8
DIFF_DIM = 64
DIFF_WIDTH = 512
DIFF_HEADS = 4
IN_WIDTH = 4 * RET_WIDTH + 3 * DIFF_WIDTH
N_BUCKETS = 32
MAX_DIST = 128
CHUNK = 128
ROPE_BASE = 10000.0
N_MOD = 9
NEG_INF = -1e30
LAMBDA_INIT = 0.8 - 0.6 * math.exp(-0.3 * 0)

V7X_LANES = 128
V7X_MXU_DIM = 256
V7X_VMEM_BYTES = 64 * 1024 * 1024

FFN_TM = 512
FFN_TF = V7X_MXU_DIM
PROJ_TM = 512
ATT_TQ = 512
ATT_TK = 1024
ATT_UNROLL = 2
ATT_ONES_ROWS = 16
RET_UNROLL = 8
LN_ROWS = 1024
LOG2E = math.log2(math.e)
SEG = 512

BF16 = jnp.bfloat16
F32 = jnp.float32


def _vmem_limit(nbytes):
    return int(min(nbytes + (12 << 20), V7X_VMEM_BYTES - (4 << 20)))


def _dot(a, b):
    return jnp.dot(a, b, preferred_element_type=F32)


def _dot_nt(a, b):
    return lax.dot_general(a, b, (((1,), (1,)), ((), ())), preferred_element_type=F32)


def _silu(x):
    return x * (1.0 / (1.0 + jnp.exp(-x)))


def _rms_mod(x, g, shift, scale):
    y = x * lax.rsqrt(jnp.mean(x * x, axis=-1, keepdims=True) + EPS)
    return (y * g) * (1.0 + scale) + shift


def _mod_kernel(c_ref, w_ref, b_ref, o_ref):
    c = c_ref[...]
    o_ref[...] = _dot(_silu(c).astype(BF16), w_ref[...].astype(BF16)) + b_ref[...]


def _mod_call(c, w_ada, b_ada):
    B = c.shape[0]
    n = N_MOD * D_MODEL
    tn = D_MODEL
    return pl.pallas_call(
        _mod_kernel,
        grid=(n // tn,),
        in_specs=[
            pl.BlockSpec((B, D_MODEL), lambda j: (0, 0)),
            pl.BlockSpec((D_MODEL, tn), lambda j: (0, j)),
            pl.BlockSpec((1, tn), lambda j: (0, j)),
        ],
        out_specs=pl.BlockSpec((B, tn), lambda j: (0, j)),
        out_shape=jax.ShapeDtypeStruct((B, n), F32),
        compiler_params=pltpu.CompilerParams(
            dimension_semantics=("parallel",),
            vmem_limit_bytes=_vmem_limit(2 * D_MODEL * tn * 4)),
        name="mod",
    )(c, w_ada, b_ada.reshape(1, n))


def _ffn_kernel(x_ref, mod_ref, g_ref, win_ref, wout_ref, gf_ref, o_ref, h_ref, act_ref, *, mod_row, final):
    x = x_ref[0]
    shift = mod_ref[0, mod_row:mod_row + 1, :]
    scale = mod_ref[0, mod_row + 1:mod_row + 2, :]
    gate = mod_ref[0, mod_row + 2:mod_row + 3, :]
    h_ref[...] = _rms_mod(x, g_ref[...], shift, scale).astype(BF16)
    for j in range(D_FF // FFN_TF):
        lo = j * FFN_TF
        g = _dot(h_ref[...], win_ref[:, lo:lo + FFN_TF])
        u = _dot(h_ref[...], win_ref[:, D_FF + lo:D_FF + lo + FFN_TF])
        act_ref[:, lo:lo + FFN_TF] = (_silu(g) * u).astype(BF16)
    y = x + (0.5 * gate) * _dot(act_ref[...], wout_ref[...])
    if final:
        y = (y * lax.rsqrt(jnp.mean(y * y, axis=-1, keepdims=True) + EPS)) * gf_ref[...]
    o_ref[0] = y


def _ffn_call(x, mod, norm_g, w_in, w_out, norm_final, *, mod_row, final):
    B, S, D = x.shape
    tm = FFN_TM
    const = lambda b, i: (0, 0)
    nbytes = (4 * tm * D * 4 + 2 * (D * 2 * D_FF + D_FF * D) * 2 + tm * D * 2 + tm * D_FF * 2)
    return pl.pallas_call(
        functools.partial(_ffn_kernel, mod_row=mod_row, final=final),
        grid=(B, S // tm),
        in_specs=[
            pl.BlockSpec((1, tm, D), lambda b, i: (b, i, 0)),
            pl.BlockSpec((1, N_MOD, D), lambda b, i: (b, 0, 0)),
            pl.BlockSpec((1, D), const),
            pl.BlockSpec((D, 2 * D_FF), const),
            pl.BlockSpec((D_FF, D), const),
            pl.BlockSpec((1, D), const),
        ],
        out_specs=pl.BlockSpec((1, tm, D), lambda b, i: (b, i, 0)),
        out_shape=jax.ShapeDtypeStruct((B, S, D), F32),
        scratch_shapes=[pltpu.VMEM((tm, D), BF16), pltpu.VMEM((tm, D_FF), BF16)],
        compiler_params=pltpu.CompilerParams(
            dimension_semantics=("parallel", "parallel"),
            vmem_limit_bytes=_vmem_limit(nbytes)),
        name="ffn_final" if final else "ffn",
    )(x, mod, norm_g.reshape(1, D), w_in, w_out, norm_final.reshape(1, D))


def _inproj_kernel(x_ref, mod_ref, g_ref, w_ref, cos_ref, sin_ref, o_ref, h_ref):
    x = x_ref[0]
    h_ref[...] = _rms_mod(x, g_ref[...], mod_ref[0, 3:4, :], mod_ref[0, 4:5, :]).astype(BF16)
    lane = lax.broadcasted_iota(jnp.int32, (1, V7X_LANES), 1)
    first_half = (lane % RET_DIM) < (RET_DIM // 2)
    cos = cos_ref[...]
    sin = sin_ref[...]
    for seg in range(IN_WIDTH // SEG):
        p = _dot(h_ref[...], w_ref[:, seg * SEG:(seg + 1) * SEG])
        if seg in (0, 1):
            post = RET_DIM ** -0.5 if seg == 1 else 1.0
            for c in range(SEG // V7X_LANES):
                v = p[:, c * V7X_LANES:(c + 1) * V7X_LANES]
                rot = jnp.where(first_half,
                                pltpu.roll(v, V7X_LANES - RET_DIM // 2, axis=1),
                                pltpu.roll(v, RET_DIM // 2, axis=1))
                r = v * cos + rot * sin
                if post != 1.0:
                    r = r * post
                o_ref[0, :, seg * SEG + c * V7X_LANES:seg * SEG + (c + 1) * V7X_LANES] = r.astype(BF16)
        elif seg == 3:
            o_ref[0, :, seg * SEG:(seg + 1) * SEG] = _silu(p).astype(BF16)
        elif seg == 4:
            o_ref[0, :, seg * SEG:(seg + 1) * SEG] = (p * (LOG2E * DIFF_DIM ** -0.5)).astype(BF16)
        else:
            o_ref[0, :, seg * SEG:(seg + 1) * SEG] = p.astype(BF16)


def _inproj_call(x, mod, norm_g, w_in, cos_tab, sin_tab):
    B, S, D = x.shape
    tm = PROJ_TM
    const = lambda b, i: (0, 0)
    nbytes = 2 * tm * D * 4 + 2 * D * IN_WIDTH * 2 + 2 * tm * IN_WIDTH * 2 + tm * D * 2 + 4 * tm * V7X_LANES * 4
    return pl.pallas_call(
        _inproj_kernel,
        grid=(B, S // tm),
        in_specs=[
            pl.BlockSpec((1, tm, D), lambda b, i: (b, i, 0)),
            pl.BlockSpec((1, N_MOD, D), lambda b, i: (b, 0, 0)),
            pl.BlockSpec((1, D), const),
            pl.BlockSpec((D, IN_WIDTH), const),
            pl.BlockSpec((tm, V7X_LANES), lambda b, i: (i, 0)),
            pl.BlockSpec((tm, V7X_LANES), lambda b, i: (i, 0)),
        ],
        out_specs=pl.BlockSpec((1, tm, IN_WIDTH), lambda b, i: (b, i, 0)),
        out_shape=jax.ShapeDtypeStruct((B, S, IN_WIDTH), BF16),
        scratch_shapes=[pltpu.VMEM((tm, D), BF16)],
        compiler_params=pltpu.CompilerParams(
            dimension_semantics=("parallel", "parallel"),
            vmem_limit_bytes=_vmem_limit(nbytes)),
        name="inproj",
    )(x, mod, norm_g.reshape(1, D), w_in, cos_tab, sin_tab)


def _bias_kernel(rb_ref, bk_ref, o_ref):
    T = ATT_TQ
    R = ATT_TK // ATT_TQ
    h = pl.program_id(0)
    far = rb_ref[N_BUCKETS - 1, h]
    near = []
    for t in range(2):
        bk = bk_ref[t]
        tile = jnp.zeros(bk.shape, F32)
        for b in range(N_BUCKETS - 1):
            tile = jnp.where(bk == b, (rb_ref[b, h] - far) * LOG2E, tile)
        near.append(jnp.where(bk < 0, NEG_INF, tile))
    zeros = jnp.zeros((T, T), F32)
    masked = jnp.full((T, T), NEG_INF, F32)
    for t in range(R + 2):
        for a in range(R):
            d = t - a if t < R else R - a if t == R else 2
            blk = masked if d < 0 else near[d] if d < 2 else zeros
            o_ref[0, t, a * T:(a + 1) * T, :] = blk


def _bias_call(rel_bias, buckets):
    T = ATT_TQ
    R = ATT_TK // ATT_TQ
    return pl.pallas_call(
        _bias_kernel,
        grid=(DIFF_HEADS,),
        in_specs=[
            pl.BlockSpec(memory_space=pltpu.SMEM),
            pl.BlockSpec((2, T, T), lambda h: (0, 0, 0)),
        ],
        out_specs=pl.BlockSpec((1, R + 2, ATT_TK, T), lambda h: (h, 0, 0, 0)),
        out_shape=jax.ShapeDtypeStruct((DIFF_HEADS, R + 2, ATT_TK, T), F32),
        compiler_params=pltpu.CompilerParams(
            dimension_semantics=("parallel",),
            vmem_limit_bytes=_vmem_limit(2 * (R + 2) * ATT_TK * T * 4)),
        name="t5bias",
    )(rel_bias, buckets)


def _ret_kernel(q_ref, k_ref, v_ref, gate_ref, dec_ref, zeta_ref, xi_ref, gc_ref, o_ref, kv_ref, r_ref, y_ref):
    C = CHUNK
    U = RET_UNROLL
    S = q_ref.shape[1]
    n_chunks = S // C
    lane = lax.broadcasted_iota(jnp.int32, (1, V7X_LANES), 1)
    m_a = jnp.where(lane < RET_DIM, 1.0, 0.0).astype(BF16)
    m_b = jnp.where(lane >= RET_DIM, 1.0, 0.0).astype(BF16)
    row = lax.broadcasted_iota(jnp.int32, (V7X_LANES, V7X_LANES), 0)
    col = lax.broadcasted_iota(jnp.int32, (V7X_LANES, V7X_LANES), 1)
    same_head = jnp.where((row < RET_DIM) == (col < RET_DIM), 1.0, 0.0)
    seg_mean = (same_head * (1.0 / RET_DIM)).astype(BF16)
    gc = gc_ref[0]

    def kv_group(g, carry):
        for u in range(U):
            n = g * U + u
            sl = pl.ds(pl.multiple_of(n * C, C), C)
            kz = (k_ref[0, sl, :].astype(F32) * zeta_ref[0]).T.astype(BF16)
            kv_ref[n] = _dot(kz, v_ref[0, sl, :]) * same_head
        return carry

    lax.fori_loop(0, n_chunks // U, kv_group, 0)

    def scan_step(n, r):
        r_ref[n] = r.astype(BF16)
        return r * gc + kv_ref[n]

    lax.fori_loop(0, n_chunks, scan_step, jnp.zeros((V7X_LANES, V7X_LANES), F32))

    def out_group(g, carry):
        for u in range(U):
            n = g * U + u
            sl = pl.ds(pl.multiple_of(n * C, C), C)
            q = q_ref[0, sl, :]
            k = k_ref[0, sl, :]
            v = v_ref[0, sl, :]
            kk = jnp.concatenate([k * m_a, k * m_b], axis=0)
            vv = jnp.concatenate([v * m_a, v * m_b], axis=0)
            s = _dot_nt(q, kk) * dec_ref[0]
            y_ref[sl, :] = _dot(s.astype(BF16), vv) + _dot(q, r_ref[n]) * xi_ref[0]
        return carry

    lax.fori_loop(0, n_chunks // U, out_group, 0)

    def ln_block(i, carry):
        sl = pl.ds(pl.multiple_of(i * LN_ROWS, LN_ROWS), LN_ROWS)
        y = y_ref[sl, :]
        d = y - _dot(y.astype(BF16), seg_mean)
        var = _dot((d * d).astype(BF16), seg_mean)
        o_ref[0, sl, :] = ((d * lax.rsqrt(var + EPS)) * gate_ref[0, sl, :].astype(F32)).astype(BF16)
        return carry

    lax.fori_loop(0, S // LN_ROWS, ln_block, 0)


def _ret_call(proj, dec, zeta, xi, gc):
    B, S, _ = proj.shape
    n_pairs = RET_HEADS // 2
    blk = lambda off: pl.BlockSpec((1, S, V7X_LANES), lambda b, p: (b, 0, off + p))
    pair = lambda shape: pl.BlockSpec((1,) + shape, lambda b, p: (p, 0, 0))
    return pl.pallas_call(
        _ret_kernel,
        grid=(B, n_pairs),
        in_specs=[
            blk(0), blk(n_pairs), blk(2 * n_pairs), blk(3 * n_pairs),
            pair((CHUNK, 2 * CHUNK)), pair((CHUNK, V7X_LANES)), pair((CHUNK, V7X_LANES)),
            pair((V7X_LANES, V7X_LANES)),
        ],
        out_specs=pl.BlockSpec((1, S, V7X_LANES), lambda b, p: (b, 0, p)),
        out_shape=jax.ShapeDtypeStruct((B, S, RET_WIDTH), BF16),
        scratch_shapes=[
            pltpu.VMEM((S // CHUNK, V7X_LANES, V7X_LANES), F32),
            pltpu.VMEM((S // CHUNK, V7X_LANES, V7X_LANES), BF16),
            pltpu.VMEM((S, V7X_LANES), F32),
        ],
        compiler_params=pltpu.CompilerParams(dimension_semantics=("parallel", "parallel")),
        name="retention",
    )(proj, proj, proj, proj, dec, zeta, xi, gc)


def _attn_steps(S):
    R = ATT_TK // ATT_TQ
    steps = []
    for qi in range(S // ATT_TQ):
        c_diag = qi // R
        for c in range(c_diag + 1):
            if c == c_diag:
                tile = qi % R
            elif c == c_diag - 1 and qi % R == 0:
                tile = R
            else:
                tile = R + 1
            steps.append((qi, c, tile, int(c == 0)))
    return steps


def _attn_kernel(steps_ref, q_ref, k_ref, v_ref, bias_ref, lq1_ref, lk1_ref, lq2_ref, lk2_ref, g_ref, o_ref,
                 vt_ref, sa_ref, sb_ref, ca_ref, cb_ref, m_ref, acc_ref, *, n_steps):
    TQ, TK = ATT_TQ, ATT_TK
    DV = 2 * DIFF_DIM
    S = q_ref.shape[1]
    lane = lax.broadcasted_iota(jnp.int32, (1, V7X_LANES), 1)
    masks = (jnp.where(lane < DIFF_DIM, 1.0, 0.0).astype(BF16),
             jnp.where(lane >= DIFF_DIM, 1.0, 0.0).astype(BF16))

    def transpose_v(c, carry):
        sl = pl.ds(pl.multiple_of(c * TK, TK), TK)
        vt_ref[c, :DV, :] = v_ref[0, sl, :].astype(F32).T.astype(BF16)
        vt_ref[c, DV:, :] = jnp.ones((ATT_ONES_ROWS, TK), BF16)
        return carry

    lax.fori_loop(0, S // TK, transpose_v, 0)

    @pl.when(jnp.logical_and(pl.program_id(0) == 0, pl.program_id(1) == 0))
    def _():
        m_ref[...] = jnp.zeros(m_ref.shape, F32)
        acc_ref[...] = jnp.zeros(acc_ref.shape, F32)

    def score(t, m, s_ref, c_ref):
        qi = steps_ref[0, t]
        c = steps_ref[1, t]
        q = q_ref[0, pl.ds(pl.multiple_of(qi * TQ, TQ), TQ), :]
        k = k_ref[0, pl.ds(pl.multiple_of(c * TK, TK), TK), :]
        s = _dot_nt(k, q * masks[m]) + bias_ref[0, steps_ref[2, t]]
        s_ref[m] = s
        c_ref[m] = jnp.max(s, axis=0, keepdims=True)

    def softmax_part(t, m, s_ref, c_ref):
        qi = steps_ref[0, t]
        restart = steps_ref[3, t] == 1
        m_old = jnp.where(restart, NEG_INF, m_ref[qi, m])
        m_new = jnp.maximum(m_old, c_ref[m])
        m_ref[qi, m] = m_new
        return jnp.exp2(s_ref[m] - m_new).astype(BF16), jnp.exp2(m_old - m_new)

    def value_part(t, m, p, alpha):
        qi = steps_ref[0, t]
        restart = steps_ref[3, t] == 1
        pv = _dot(vt_ref[steps_ref[1, t]], p)
        acc_ref[qi, m] = jnp.where(restart, 0.0, alpha * acc_ref[qi, m]) + pv

    def step(t, cur, nxt):
        score(t + 1, 0, *nxt)
        p0, a0 = softmax_part(t, 0, *cur)
        score(t + 1, 1, *nxt)
        value_part(t, 0, p0, a0)
        p1, a1 = softmax_part(t, 1, *cur)
        value_part(t, 1, p1, a1)

    buf_a = (sa_ref, ca_ref)
    buf_b = (sb_ref, cb_ref)
    score(0, 0, *buf_a)
    score(0, 1, *buf_a)

    U = ATT_UNROLL
    def step_group(i, carry):
        for u in range(0, U, 2):
            step(U * i + u, buf_a, buf_b)
            step(U * i + u + 1, buf_b, buf_a)
        return carry

    lax.fori_loop(0, n_steps // U, step_group, 0)
    for t in range(n_steps - n_steps % U, n_steps):
        step(t, *((buf_a, buf_b) if t % 2 == 0 else (buf_b, buf_a)))

    lam = (jnp.exp(jnp.sum(lq1_ref[...] * lk1_ref[...], axis=-1, keepdims=True))
           - jnp.exp(jnp.sum(lq2_ref[...] * lk2_ref[...], axis=-1, keepdims=True)) + LAMBDA_INIT)

    def finalize(qi, carry):
        a0 = acc_ref[qi, 0]
        a1 = acc_ref[qi, 1]
        o = a0[:DV] / a0[DV:DV + 1] - lam * (a1[:DV] / a1[DV:DV + 1])
        y = o * lax.rsqrt(jnp.mean(o * o, axis=0, keepdims=True) + EPS)
        y = (y * g_ref[...]) * (1.0 - LAMBDA_INIT)
        o_ref[0, pl.ds(pl.multiple_of(qi * TQ, TQ), TQ), :] = y.T.astype(BF16)
        return carry

    lax.fori_loop(0, S // TQ, finalize, 0)


def _attn_call(proj, bias_tiles, lq1, lk1, lq2, lk2, subln_g):
    B, S, _ = proj.shape
    TQ, TK = ATT_TQ, ATT_TK
    n_tiles = bias_tiles.shape[1]
    n_q = S // TQ
    steps = _attn_steps(S)
    table = jnp.asarray(np.array(steps + [steps[-1]], np.int32).T)
    q_off = 4 * RET_WIDTH // V7X_LANES
    rows = 2 * DIFF_DIM + ATT_ONES_ROWS
    nbytes = (8 * S * V7X_LANES * 2 + 2 * n_tiles * TK * TQ * 4 + S * rows * 2
              + n_q * 2 * rows * TQ * 4 + 4 * TK * TQ * 4 + 3 * TK * TQ * 4)
    blk = lambda off: pl.BlockSpec((1, S, V7X_LANES), lambda h, b: (b, 0, off + h))
    vec = pl.BlockSpec((1, DIFF_DIM), lambda h, b: (0, 0))
    return pl.pallas_call(
        functools.partial(_attn_kernel, n_steps=len(steps)),
        grid=(DIFF_HEADS, B),
        in_specs=[
            pl.BlockSpec(memory_space=pltpu.SMEM),
            blk(q_off), blk(q_off + DIFF_HEADS), blk(q_off + 2 * DIFF_HEADS),
            pl.BlockSpec((1, n_tiles, TK, TQ), lambda h, b: (h, 0, 0, 0)),
            vec, vec, vec, vec,
            pl.BlockSpec((2 * DIFF_DIM, 1), lambda h, b: (0, 0)),
        ],
        out_specs=pl.BlockSpec((1, S, V7X_LANES), lambda h, b: (b, 0, h)),
        out_shape=jax.ShapeDtypeStruct((B, S, DIFF_WIDTH), BF16),
        scratch_shapes=[
            pltpu.VMEM((S // TK, rows, TK), BF16),
            pltpu.VMEM((2, TK, TQ), F32),
            pltpu.VMEM((2, TK, TQ), F32),
            pltpu.VMEM((2, 1, TQ), F32),
            pltpu.VMEM((2, 1, TQ), F32),
            pltpu.VMEM((n_q, 2, 1, TQ), F32),
            pltpu.VMEM((n_q, 2, rows, TQ), F32),
        ],
        compiler_params=pltpu.CompilerParams(
            dimension_semantics=("arbitrary", "arbitrary"),
            vmem_limit_bytes=_vmem_limit(nbytes)),
        name="diffattn",
    )(table, proj, proj, proj, bias_tiles, lq1, lk1, lq2, lk2, subln_g.reshape(2 * DIFF_DIM, 1))


def _outproj_kernel(x_ref, mod_ref, yr_ref, yd_ref, gs_ref, w_ref, o_ref):
    gs = gs_ref[...]
    ya = (yr_ref[0].astype(F32) * gs[:, :RET_WIDTH]).astype(BF16)
    yb = (yd_ref[0].astype(F32) * gs[:, RET_WIDTH:]).astype(BF16)
    mix = _dot(ya, w_ref[:RET_WIDTH, :]) + _dot(yb, w_ref[RET_WIDTH:, :])
    o_ref[0] = x_ref[0] + mod_ref[0, 5:6, :] * mix


def _outproj_call(x, mod, y_ret, y_diff, group_scale, w_out):
    B, S, D = x.shape
    tm = PROJ_TM
    const = lambda b, i: (0, 0)
    nbytes = 4 * tm * D * 4 + 2 * D * D * 2 + 4 * tm * RET_WIDTH * 2
    return pl.pallas_call(
        _outproj_kernel,
        grid=(B, S // tm),
        in_specs=[
            pl.BlockSpec((1, tm, D), lambda b, i: (b, i, 0)),
            pl.BlockSpec((1, N_MOD, D), lambda b, i: (b, 0, 0)),
            pl.BlockSpec((1, tm, RET_WIDTH), lambda b, i: (b, i, 0)),
            pl.BlockSpec((1, tm, DIFF_WIDTH), lambda b, i: (b, i, 0)),
            pl.BlockSpec((1, D), const),
            pl.BlockSpec((D, D), const),
        ],
        out_specs=pl.BlockSpec((1, tm, D), lambda b, i: (b, i, 0)),
        out_shape=jax.ShapeDtypeStruct((B, S, D), F32),
        compiler_params=pltpu.CompilerParams(
            dimension_semantics=("parallel", "parallel"),
            vmem_limit_bytes=_vmem_limit(nbytes)),
        name="outproj",
    )(x, mod, y_ret, y_diff, group_scale.reshape(1, D), w_out)


def _rope_tables(S):
    half = RET_DIM // 2
    inv = ROPE_BASE ** (-jnp.arange(0, RET_DIM, 2, dtype=F32) / RET_DIM)
    ang = jnp.arange(S).astype(F32)[:, None] * inv[None, :]
    cos = jnp.cos(ang)
    sin = jnp.sin(ang)
    reps = V7X_LANES // RET_DIM
    cos_tab = jnp.tile(jnp.concatenate([cos, cos], axis=1), (1, reps))
    sin_tab = jnp.tile(jnp.concatenate([-sin, sin], axis=1), (1, reps))
    del half
    return cos_tab, sin_tab


def _retention_tables():
    C = CHUNK
    log_gamma = jnp.log1p(-(2.0 ** (-5.0 - jnp.arange(RET_HEADS, dtype=F32))))
    idx = jnp.arange(C, dtype=F32)
    dist = idx[:, None] - idx[None, :]
    decay = jnp.where(dist >= 0, jnp.exp(log_gamma[:, None, None] * jnp.maximum(dist, 0.0)[None]), 0.0)
    zeta = jnp.exp(log_gamma[:, None] * (C - 1 - idx)[None])
    xi = jnp.exp(log_gamma[:, None] * (idx + 1.0)[None])
    gamma_c = jnp.exp(log_gamma * C)
    n_pairs = RET_HEADS // 2
    dec = decay.reshape(n_pairs, 2, C, C).transpose(0, 2, 1, 3).reshape(n_pairs, C, 2 * C)
    lanes = lambda t: jnp.repeat(t.reshape(n_pairs, 2, C).transpose(0, 2, 1), RET_DIM, axis=2)
    g_rows = jnp.repeat(gamma_c.reshape(n_pairs, 2), RET_DIM, axis=1)
    blockdiag = (jnp.arange(V7X_LANES)[:, None] < RET_DIM) == (jnp.arange(V7X_LANES)[None, :] < RET_DIM)
    gc = jnp.where(blockdiag[None], g_rows[:, :, None], 0.0)
    return dec, lanes(zeta), lanes(xi), gc


def _t5_bucket(rel):
    n = jnp.maximum(rel, 0)
    max_exact = N_BUCKETS // 2
    nf = jnp.maximum(n, 1).astype(F32)
    large = max_exact + (jnp.log(nf / max_exact) / math.log(MAX_DIST / max_exact)
                         * (N_BUCKETS - max_exact)).astype(jnp.int32)
    large = jnp.minimum(large, N_BUCKETS - 1)
    return jnp.where(n < max_exact, n, large)


def _bucket_tiles():
    T = ATT_TQ
    kpos = jnp.arange(T)[:, None]
    qpos = jnp.arange(T)[None, :]
    rel0 = qpos - kpos
    rel1 = rel0 + T
    t0 = jnp.where(rel0 >= 0, _t5_bucket(rel0), -1)
    return jnp.stack([t0, _t5_bucket(rel1)]).astype(jnp.int32)


def kernel(x, c, w_ada, b_ada, norm_ffn1, w_ffn1_in, w_ffn1_out, norm_mix, w_in, lambda_q1, lambda_k1,
           lambda_q2, lambda_k2, subln_gain, group_scale, w_out, norm_ffn2, w_ffn2_in, w_ffn2_out, rel_bias,
           norm_final):
    B, S, D = x.shape
    l = 0
    mod = _mod_call(c, w_ada[l], b_ada[l]).reshape(B, N_MOD, D)

    x = _ffn_call(x, mod, norm_ffn1[l], w_ffn1_in[l].astype(BF16), w_ffn1_out[l].astype(BF16), norm_final,
                  mod_row=0, final=False)

    cos_tab, sin_tab = _rope_tables(S)
    proj = _inproj_call(x, mod, norm_mix[l], w_in[l].astype(BF16), cos_tab, sin_tab)
    y_ret = _ret_call(proj, *_retention_tables())
    bias_tiles = _bias_call(rel_bias, _bucket_tiles())
    y_diff = _attn_call(proj, bias_tiles, lambda_q1[l][None], lambda_k1[l][None], lambda_q2[l][None],
                        lambda_k2[l][None], subln_gain[l])
    x = _outproj_call(x, mod, y_ret, y_diff, group_scale[l], w_out[l].astype(BF16))

    return _ffn_call(x, mod, norm_ffn2[l], w_ffn2_in[l].astype(BF16), w_ffn2_out[l].astype(BF16), norm_final,
                     mod_row=6, final=True)
```

```python
import functools
import math

import jax
import jax.numpy as jnp
import numpy as np
from jax import lax
from jax.experimental import pallas as pl
from jax.experimental.pallas import tpu as pltpu

D_MODEL = 1024
D_FF = 2816
EPS = 1e-6
RET_DIM = 64
RET_WIDTH = 512
RET_HEADS = 8
DIFF_DIM = 64
DIFF_WIDTH = 512
DIFF_HEADS = 4
IN_WIDTH = 4 * RET_WIDTH + 3 * DIFF_WIDTH
N_BUCKETS = 32
MAX_DIST = 128
CHUNK = 128
ROPE_BASE = 10000.0
N_MOD = 9
NEG_INF = -1e30
LAMBDA_INIT = 0.8 - 0.6 * math.exp(-0.3 * 0)

V7X_LANES = 128
V7X_MXU_DIM = 256
V7X_VMEM_BYTES = 64 * 1024 * 1024

FFN_TM = 1024
FFN_TS = 256
FFN_TF = V7X_MXU_DIM
PROJ_TM = 512
ATT_TQ = 512
ATT_TK = 1024
ATT_UNROLL = 2
ATT_ONES_ROWS = 16
RET_UNROLL = 8
LN_ROWS = 1024
LOG2E = math.log2(math.e)
SEG = 512

BF16 = jnp.bfloat16
F32 = jnp.float32


def _vmem_limit(nbytes):
    return int(min(nbytes + (12 << 20), V7X_VMEM_BYTES - (4 << 20)))


def _dot(a, b):
    return jnp.dot(a, b, preferred_element_type=F32)


def _dot_nt(a, b):
    return lax.dot_general(a, b, (((1,), (1,)), ((), ())), preferred_element_type=F32)


def _silu(x):
    return x * (1.0 / (1.0 + jnp.exp(-x)))


def _rms_mod(x, g, shift, scale):
    y = x * lax.rsqrt(jnp.mean(x * x, axis=-1, keepdims=True) + EPS)
    return (y * g) * (1.0 + scale) + shift


def _mod_kernel(c_ref, w_ref, b_ref, o_ref):
    c = c_ref[...]
    o_ref[...] = _dot(_silu(c).astype(BF16), w_ref[...].astype(BF16)) + b_ref[...]


def _mod_call(c, w_ada, b_ada):
    B = c.shape[0]
    n = N_MOD * D_MODEL
    tn = D_MODEL
    return pl.pallas_call(
        _mod_kernel,
        grid=(n // tn,),
        in_specs=[
            pl.BlockSpec((B, D_MODEL), lambda j: (0, 0)),
            pl.BlockSpec((D_MODEL, tn), lambda j: (0, j)),
            pl.BlockSpec((1, tn), lambda j: (0, j)),
        ],
        out_specs=pl.BlockSpec((B, tn), lambda j: (0, j)),
        out_shape=jax.ShapeDtypeStruct((B, n), F32),
        compiler_params=pltpu.CompilerParams(
            dimension_semantics=("parallel",),
            vmem_limit_bytes=_vmem_limit(2 * D_MODEL * tn * 4)),
        name="mod",
    )(c, w_ada, b_ada.reshape(1, n))


def _ffn_kernel(*refs, mod_row, mix):
    if mix:
        (x_ref, mod_ref, g_ref, win_ref, wout_ref, yr_ref, yd_ref, gs_ref, wmix_ref, gf_ref,
         o_ref, h_ref, act_ref) = refs
    else:
        x_ref, mod_ref, g_ref, win_ref, wout_ref, o_ref, h_ref, act_ref = refs
    shift = mod_ref[0, mod_row:mod_row + 1, :]
    scale = mod_ref[0, mod_row + 1:mod_row + 2, :]
    gate = mod_ref[0, mod_row + 2:mod_row + 3, :]
    n_sub = FFN_TM // FFN_TS

    def prologue(sub):
        rows = slice(sub * FFN_TS, (sub + 1) * FFN_TS)
        x = x_ref[0, rows, :]
        if mix:
            gs = gs_ref[...]
            ya = (yr_ref[0, rows, :].astype(F32) * gs[:, :RET_WIDTH]).astype(BF16)
            yb = (yd_ref[0, rows, :].astype(F32) * gs[:, RET_WIDTH:]).astype(BF16)
            x = x + mod_ref[0, 5:6, :] * (_dot(ya, wmix_ref[:RET_WIDTH, :]) + _dot(yb, wmix_ref[RET_WIDTH:, :]))
        h_ref[rows, :] = _rms_mod(x, g_ref[...], shift, scale).astype(BF16)
        return x

    x_next = prologue(0)
    for sub in range(n_sub):
        rows = slice(sub * FFN_TS, (sub + 1) * FFN_TS)
        x = x_next
        if sub + 1 < n_sub:
            x_next = prologue(sub + 1)
        for j in range(D_FF // FFN_TF):
            lo = j * FFN_TF
            g = _dot(h_ref[rows, :], win_ref[:, lo:lo + FFN_TF])
            u = _dot(h_ref[rows, :], win_ref[:, D_FF + lo:D_FF + lo + FFN_TF])
            act_ref[rows, lo:lo + FFN_TF] = (_silu(g) * u).astype(BF16)
        y = x + (0.5 * gate) * _dot(act_ref[rows, :], wout_ref[...])
        if mix:
            y = (y * lax.rsqrt(jnp.mean(y * y, axis=-1, keepdims=True) + EPS)) * gf_ref[...]
        o_ref[0, rows, :] = y


def _ffn_call(x, mod, norm_g, w_in, w_out, mixer=None, *, mod_row):
    B, S, D = x.shape
    tm = FFN_TM
    const = lambda b, i: (0, 0)
    rows = lambda width: pl.BlockSpec((1, tm, width), lambda b, i: (b, i, 0))
    resident = lambda shape: pl.BlockSpec(shape, const, pipeline_mode=pl.Buffered(1))
    in_specs = [
        rows(D),
        pl.BlockSpec((1, N_MOD, D), lambda b, i: (b, 0, 0)),
        pl.BlockSpec((1, D), const),
        resident((D, 2 * D_FF)),
        resident((D_FF, D)),
    ]
    args = [x, mod, norm_g.reshape(1, D), w_in, w_out]
    nbytes = 4 * tm * D * 4 + (D * 2 * D_FF + D_FF * D) * 2 + tm * D * 2 + tm * D_FF * 2
    if mixer is not None:
        y_ret, y_diff, group_scale, w_mix, norm_final = mixer
        in_specs += [rows(RET_WIDTH), rows(DIFF_WIDTH), pl.BlockSpec((1, D), const), resident((D, D)),
                     pl.BlockSpec((1, D), const)]
        args += [y_ret, y_diff, group_scale.reshape(1, D), w_mix, norm_final.reshape(1, D)]
        nbytes += 4 * tm * RET_WIDTH * 2 + D * D * 2
    return pl.pallas_call(
        functools.partial(_ffn_kernel, mod_row=mod_row, mix=mixer is not None),
        grid=(B, S // tm),
        in_specs=in_specs,
        out_specs=rows(D),
        out_shape=jax.ShapeDtypeStruct((B, S, D), F32),
        scratch_shapes=[pltpu.VMEM((tm, D), BF16), pltpu.VMEM((tm, D_FF), BF16)],
        compiler_params=pltpu.CompilerParams(
            dimension_semantics=("parallel", "parallel"),
            vmem_limit_bytes=_vmem_limit(nbytes)),
        name="mix_ffn_final" if mixer is not None else "ffn",
    )(*args)


def _inproj_kernel(x_ref, mod_ref, g_ref, w_ref, cos_ref, sin_ref, o_ref, h_ref):
    x = x_ref[0]
    h_ref[...] = _rms_mod(x, g_ref[...], mod_ref[0, 3:4, :], mod_ref[0, 4:5, :]).astype(BF16)
    lane = lax.broadcasted_iota(jnp.int32, (1, V7X_LANES), 1)
    first_half = (lane % RET_DIM) < (RET_DIM // 2)
    cos = cos_ref[...]
    sin = sin_ref[...]
    for seg in range(IN_WIDTH // SEG):
        p = _dot(h_ref[...], w_ref[:, seg * SEG:(seg + 1) * SEG])
        if seg in (0, 1):
            post = RET_DIM ** -0.5 if seg == 1 else 1.0
            for c in range(SEG // V7X_LANES):
                v = p[:, c * V7X_LANES:(c + 1) * V7X_LANES]
                rot = jnp.where(first_half,
                                pltpu.roll(v, V7X_LANES - RET_DIM // 2, axis=1),
                                pltpu.roll(v, RET_DIM // 2, axis=1))
                r = v * cos + rot * sin
                if post != 1.0:
                    r = r * post
                o_ref[0, :, seg * SEG + c * V7X_LANES:seg * SEG + (c + 1) * V7X_LANES] = r.astype(BF16)
        elif seg == 3:
            o_ref[0, :, seg * SEG:(seg + 1) * SEG] = _silu(p).astype(BF16)
        elif seg == 4:
            o_ref[0, :, seg * SEG:(seg + 1) * SEG] = (p * (LOG2E * DIFF_DIM ** -0.5)).astype(BF16)
        else:
            o_ref[0, :, seg * SEG:(seg + 1) * SEG] = p.astype(BF16)


def _inproj_call(x, mod, norm_g, w_in, cos_tab, sin_tab):
    B, S, D = x.shape
    tm = PROJ_TM
    const = lambda b, i: (0, 0)
    nbytes = 2 * tm * D * 4 + 2 * D * IN_WIDTH * 2 + 2 * tm * IN_WIDTH * 2 + tm * D * 2 + 4 * tm * V7X_LANES * 4
    return pl.pallas_call(
        _inproj_kernel,
        grid=(B, S // tm),
        in_specs=[
            pl.BlockSpec((1, tm, D), lambda b, i: (b, i, 0)),
            pl.BlockSpec((1, N_MOD, D), lambda b, i: (b, 0, 0)),
            pl.BlockSpec((1, D), const),
            pl.BlockSpec((D, IN_WIDTH), const),
            pl.BlockSpec((tm, V7X_LANES), lambda b, i: (i, 0)),
            pl.BlockSpec((tm, V7X_LANES), lambda b, i: (i, 0)),
        ],
        out_specs=pl.BlockSpec((1, tm, IN_WIDTH), lambda b, i: (b, i, 0)),
        out_shape=jax.ShapeDtypeStruct((B, S, IN_WIDTH), BF16),
        scratch_shapes=[pltpu.VMEM((tm, D), BF16)],
        compiler_params=pltpu.CompilerParams(
            dimension_semantics=("parallel", "parallel"),
            vmem_limit_bytes=_vmem_limit(nbytes)),
        name="inproj",
    )(x, mod, norm_g.reshape(1, D), w_in, cos_tab, sin_tab)


def _bias_kernel(rb_ref, bk_ref, o_ref):
    T = ATT_TQ
    R = ATT_TK // ATT_TQ
    h = pl.program_id(0)
    far = rb_ref[N_BUCKETS - 1, h]
    near = []
    for t in range(2):
        bk = bk_ref[t]
        tile = jnp.zeros(bk.shape, F32)
        for b in range(N_BUCKETS - 1):
            tile = jnp.where(bk == b, (rb_ref[b, h] - far) * LOG2E, tile)
        near.append(jnp.where(bk < 0, NEG_INF, tile))
    zeros = jnp.zeros((T, T), F32)
    masked = jnp.full((T, T), NEG_INF, F32)
    for t in range(R + 2):
        for a in range(R):
            d = t - a if t < R else R - a if t == R else 2
            blk = masked if d < 0 else near[d] if d < 2 else zeros
            o_ref[0, t, a * T:(a + 1) * T, :] = blk


def _bias_call(rel_bias, buckets):
    T = ATT_TQ
    R = ATT_TK // ATT_TQ
    return pl.pallas_call(
        _bias_kernel,
        grid=(DIFF_HEADS,),
        in_specs=[
            pl.BlockSpec(memory_space=pltpu.SMEM),
            pl.BlockSpec((2, T, T), lambda h: (0, 0, 0)),
        ],
        out_specs=pl.BlockSpec((1, R + 2, ATT_TK, T), lambda h: (h, 0, 0, 0)),
        out_shape=jax.ShapeDtypeStruct((DIFF_HEADS, R + 2, ATT_TK, T), F32),
        compiler_params=pltpu.CompilerParams(
            dimension_semantics=("parallel",),
            vmem_limit_bytes=_vmem_limit(2 * (R + 2) * ATT_TK * T * 4)),
        name="t5bias",
    )(rel_bias, buckets)


def _ret_kernel(q_ref, k_ref, v_ref, gate_ref, dec_ref, zeta_ref, xi_ref, gc_ref, o_ref, kv_ref, r_ref, y_ref):
    C = CHUNK
    U = RET_UNROLL
    S = q_ref.shape[1]
    n_chunks = S // C
    lane = lax.broadcasted_iota(jnp.int32, (1, V7X_LANES), 1)
    m_a = jnp.where(lane < RET_DIM, 1.0, 0.0).astype(BF16)
    m_b = jnp.where(lane >= RET_DIM, 1.0, 0.0).astype(BF16)
    row = lax.broadcasted_iota(jnp.int32, (V7X_LANES, V7X_LANES), 0)
    col = lax.broadcasted_iota(jnp.int32, (V7X_LANES, V7X_LANES), 1)
    same_head = jnp.where((row < RET_DIM) == (col < RET_DIM), 1.0, 0.0)
    seg_mean = (same_head * (1.0 / RET_DIM)).astype(BF16)
    gc = gc_ref[0]

    def kv_group(g, carry):
        for u in range(U):
            n = g * U + u
            sl = pl.ds(pl.multiple_of(n * C, C), C)
            kz = (k_ref[0, sl, :].astype(F32) * zeta_ref[0]).T.astype(BF16)
            kv_ref[n] = _dot(kz, v_ref[0, sl, :]) * same_head
        return carry

    lax.fori_loop(0, n_chunks // U, kv_group, 0)

    def scan_step(n, r):
        r_ref[n] = r.astype(BF16)
        return r * gc + kv_ref[n]

    lax.fori_loop(0, n_chunks, scan_step, jnp.zeros((V7X_LANES, V7X_LANES), F32))

    def out_group(g, carry):
        for u in range(U):
            n = g * U + u
            sl = pl.ds(pl.multiple_of(n * C, C), C)
            q = q_ref[0, sl, :]
            k = k_ref[0, sl, :]
            v = v_ref[0, sl, :]
            kk = jnp.concatenate([k * m_a, k * m_b], axis=0)
            vv = jnp.concatenate([v * m_a, v * m_b], axis=0)
            s = _dot_nt(q, kk) * dec_ref[0]
            y_ref[sl, :] = _dot(s.astype(BF16), vv) + _dot(q, r_ref[n]) * xi_ref[0]
        return carry

    lax.fori_loop(0, n_chunks // U, out_group, 0)

    def ln_block(i, carry):
        sl = pl.ds(pl.multiple_of(i * LN_ROWS, LN_ROWS), LN_ROWS)
        y = y_ref[sl, :]
        d = y - _dot(y.astype(BF16), seg_mean)
        var = _dot((d * d).astype(BF16), seg_mean)
        o_ref[0, sl, :] = ((d * lax.rsqrt(var + EPS)) * gate_ref[0, sl, :].astype(F32)).astype(BF16)
        return carry

    lax.fori_loop(0, S // LN_ROWS, ln_block, 0)


def _ret_call(proj, dec, zeta, xi, gc):
    B, S, _ = proj.shape
    n_pairs = RET_HEADS // 2
    blk = lambda off: pl.BlockSpec((1, S, V7X_LANES), lambda b, p: (b, 0, off + p))
    pair = lambda shape: pl.BlockSpec((1,) + shape, lambda b, p: (p, 0, 0))
    return pl.pallas_call(
        _ret_kernel,
        grid=(B, n_pairs),
        in_specs=[
            blk(0), blk(n_pairs), blk(2 * n_pairs), blk(3 * n_pairs),
            pair((CHUNK, 2 * CHUNK)), pair((CHUNK, V7X_LANES)), pair((CHUNK, V7X_LANES)),
            pair((V7X_LANES, V7X_LANES)),
        ],
        out_specs=pl.BlockSpec((1, S, V7X_LANES), lambda b, p: (b, 0, p)),
        out_shape=jax.ShapeDtypeStruct((B, S, RET_WIDTH), BF16),
        scratch_shapes=[
            pltpu.VMEM((S // CHUNK, V7X_LANES, V7X_LANES), F32),
            pltpu.VMEM((S // CHUNK, V7X_LANES, V7X_LANES), BF16),
            pltpu.VMEM((S, V7X_LANES), F32),
        ],
        compiler_params=pltpu.CompilerParams(dimension_semantics=("parallel", "parallel")),
        name="retention",
    )(proj, proj, proj, proj, dec, zeta, xi, gc)


def _attn_steps(S):
    R = ATT_TK // ATT_TQ
    steps = []
    for qi in range(S // ATT_TQ):
        c_diag = qi // R
        for c in range(c_diag + 1):
            if c == c_diag:
                tile = qi % R
            elif c == c_diag - 1 and qi % R == 0:
                tile = R
            else:
                tile = R + 1
            steps.append((qi, c, tile, int(c == 0)))
    return steps


def _attn_kernel(steps_ref, q_ref, k_ref, v_ref, bias_ref, lq1_ref, lk1_ref, lq2_ref, lk2_ref, g_ref, o_ref,
                 vt_ref, sa_ref, sb_ref, ca_ref, cb_ref, m_ref, acc_ref, *, n_steps):
    TQ, TK = ATT_TQ, ATT_TK
    DV = 2 * DIFF_DIM
    S = q_ref.shape[1]
    lane = lax.broadcasted_iota(jnp.int32, (1, V7X_LANES), 1)
    masks = (jnp.where(lane < DIFF_DIM, 1.0, 0.0).astype(BF16),
             jnp.where(lane >= DIFF_DIM, 1.0, 0.0).astype(BF16))

    def transpose_v(c, carry):
        sl = pl.ds(pl.multiple_of(c * TK, TK), TK)
        vt_ref[c, :DV, :] = v_ref[0, sl, :].astype(F32).T.astype(BF16)
        vt_ref[c, DV:, :] = jnp.ones((ATT_ONES_ROWS, TK), BF16)
        return carry

    lax.fori_loop(0, S // TK, transpose_v, 0)

    @pl.when(jnp.logical_and(pl.program_id(0) == 0, pl.program_id(1) == 0))
    def _():
        m_ref[...] = jnp.zeros(m_ref.shape, F32)
        acc_ref[...] = jnp.zeros(acc_ref.shape, F32)

    def score(t, m, s_ref, c_ref):
        qi = steps_ref[0, t]
        c = steps_ref[1, t]
        q = q_ref[0, pl.ds(pl.multiple_of(qi * TQ, TQ), TQ), :]
        k = k_ref[0, pl.ds(pl.multiple_of(c * TK, TK), TK), :]
        s = _dot_nt(k, q * masks[m]) + bias_ref[0, steps_ref[2, t]]
        s_ref[m] = s
        c_ref[m] = jnp.max(s, axis=0, keepdims=True)

    def softmax_part(t, m, s_ref, c_ref):
        qi = steps_ref[0, t]
        restart = steps_ref[3, t] == 1
        m_old = jnp.where(restart, NEG_INF, m_ref[qi, m])
        m_new = jnp.maximum(m_old, c_ref[m])
        m_ref[qi, m] = m_new
        return jnp.exp2(s_ref[m] - m_new).astype(BF16), jnp.exp2(m_old - m_new)

    def value_part(t, m, p, alpha):
        qi = steps_ref[0, t]
        restart = steps_ref[3, t] == 1
        pv = _dot(vt_ref[steps_ref[1, t]], p)
        acc_ref[qi, m] = jnp.where(restart, 0.0, alpha * acc_ref[qi, m]) + pv

    def step(t, cur, nxt):
        score(t + 1, 0, *nxt)
        p0, a0 = softmax_part(t, 0, *cur)
        score(t + 1, 1, *nxt)
        value_part(t, 0, p0, a0)
        p1, a1 = softmax_part(t, 1, *cur)
        value_part(t, 1, p1, a1)

    buf_a = (sa_ref, ca_ref)
    buf_b = (sb_ref, cb_ref)
    score(0, 0, *buf_a)
    score(0, 1, *buf_a)

    U = ATT_UNROLL
    def step_group(i, carry):
        for u in range(0, U, 2):
            step(U * i + u, buf_a, buf_b)
            step(U * i + u + 1, buf_b, buf_a)
        return carry

    lax.fori_loop(0, n_steps // U, step_group, 0)
    for t in range(n_steps - n_steps % U, n_steps):
        step(t, *((buf_a, buf_b) if t % 2 == 0 else (buf_b, buf_a)))

    lam = (jnp.exp(jnp.sum(lq1_ref[...] * lk1_ref[...], axis=-1, keepdims=True))
           - jnp.exp(jnp.sum(lq2_ref[...] * lk2_ref[...], axis=-1, keepdims=True)) + LAMBDA_INIT)

    def finalize(qi, carry):
        a0 = acc_ref[qi, 0]
        a1 = acc_ref[qi, 1]
        o = a0[:DV] / a0[DV:DV + 1] - lam * (a1[:DV] / a1[DV:DV + 1])
        y = o * lax.rsqrt(jnp.mean(o * o, axis=0, keepdims=True) + EPS)
        y = (y * g_ref[...]) * (1.0 - LAMBDA_INIT)
        o_ref[0, pl.ds(pl.multiple_of(qi * TQ, TQ), TQ), :] = y.T.astype(BF16)
        return carry

    lax.fori_loop(0, S // TQ, finalize, 0)


def _attn_call(proj, bias_tiles, lq1, lk1, lq2, lk2, subln_g):
    B, S, _ = proj.shape
    TQ, TK = ATT_TQ, ATT_TK
    n_tiles = bias_tiles.shape[1]
    n_q = S // TQ
    steps = _attn_steps(S)
    table = jnp.asarray(np.array(steps + [steps[-1]], np.int32).T)
    q_off = 4 * RET_WIDTH // V7X_LANES
    rows = 2 * DIFF_DIM + ATT_ONES_ROWS
    nbytes = (8 * S * V7X_LANES * 2 + 2 * n_tiles * TK * TQ * 4 + S * rows * 2
              + n_q * 2 * rows * TQ * 4 + 4 * TK * TQ * 4 + 3 * TK * TQ * 4)
    blk = lambda off: pl.BlockSpec((1, S, V7X_LANES), lambda h, b: (b, 0, off + h))
    vec = pl.BlockSpec((1, DIFF_DIM), lambda h, b: (0, 0))
    return pl.pallas_call(
        functools.partial(_attn_kernel, n_steps=len(steps)),
        grid=(DIFF_HEADS, B),
        in_specs=[
            pl.BlockSpec(memory_space=pltpu.SMEM),
            blk(q_off), blk(q_off + DIFF_HEADS), blk(q_off + 2 * DIFF_HEADS),
            pl.BlockSpec((1, n_tiles, TK, TQ), lambda h, b: (h, 0, 0, 0)),
            vec, vec, vec, vec,
            pl.BlockSpec((2 * DIFF_DIM, 1), lambda h, b: (0, 0)),
        ],
        out_specs=pl.BlockSpec((1, S, V7X_LANES), lambda h, b: (b, 0, h)),
        out_shape=jax.ShapeDtypeStruct((B, S, DIFF_WIDTH), BF16),
        scratch_shapes=[
            pltpu.VMEM((S // TK, rows, TK), BF16),
            pltpu.VMEM((2, TK, TQ), F32),
            pltpu.VMEM((2, TK, TQ), F32),
            pltpu.VMEM((2, 1, TQ), F32),
            pltpu.VMEM((2, 1, TQ), F32),
            pltpu.VMEM((n_q, 2, 1, TQ), F32),
            pltpu.VMEM((n_q, 2, rows, TQ), F32),
        ],
        compiler_params=pltpu.CompilerParams(
            dimension_semantics=("arbitrary", "arbitrary"),
            vmem_limit_bytes=_vmem_limit(nbytes)),
        name="diffattn",
    )(table, proj, proj, proj, bias_tiles, lq1, lk1, lq2, lk2, subln_g.reshape(2 * DIFF_DIM, 1))


def _rope_tables(S):
    half = RET_DIM // 2
    inv = ROPE_BASE ** (-jnp.arange(0, RET_DIM, 2, dtype=F32) / RET_DIM)
    ang = jnp.arange(S).astype(F32)[:, None] * inv[None, :]
    cos = jnp.cos(ang)
    sin = jnp.sin(ang)
    reps = V7X_LANES // RET_DIM
    cos_tab = jnp.tile(jnp.concatenate([cos, cos], axis=1), (1, reps))
    sin_tab = jnp.tile(jnp.concatenate([-sin, sin], axis=1), (1, reps))
    del half
    return cos_tab, sin_tab


def _retention_tables():
    C = CHUNK
    log_gamma = jnp.log1p(-(2.0 ** (-5.0 - jnp.arange(RET_HEADS, dtype=F32))))
    idx = jnp.arange(C, dtype=F32)
    dist = idx[:, None] - idx[None, :]
    decay = jnp.where(dist >= 0, jnp.exp(log_gamma[:, None, None] * jnp.maximum(dist, 0.0)[None]), 0.0)
    zeta = jnp.exp(log_gamma[:, None] * (C - 1 - idx)[None])
    xi = jnp.exp(log_gamma[:, None] * (idx + 1.0)[None])
    gamma_c = jnp.exp(log_gamma * C)
    n_pairs = RET_HEADS // 2
    dec = decay.reshape(n_pairs, 2, C, C).transpose(0, 2, 1, 3).reshape(n_pairs, C, 2 * C)
    lanes = lambda t: jnp.repeat(t.reshape(n_pairs, 2, C).transpose(0, 2, 1), RET_DIM, axis=2)
    g_rows = jnp.repeat(gamma_c.reshape(n_pairs, 2), RET_DIM, axis=1)
    blockdiag = (jnp.arange(V7X_LANES)[:, None] < RET_DIM) == (jnp.arange(V7X_LANES)[None, :] < RET_DIM)
    gc = jnp.where(blockdiag[None], g_rows[:, :, None], 0.0)
    return dec, lanes(zeta), lanes(xi), gc


def _t5_bucket(rel):
    n = jnp.maximum(rel, 0)
    max_exact = N_BUCKETS // 2
    nf = jnp.maximum(n, 1).astype(F32)
    large = max_exact + (jnp.log(nf / max_exact) / math.log(MAX_DIST / max_exact)
                         * (N_BUCKETS - max_exact)).astype(jnp.int32)
    large = jnp.minimum(large, N_BUCKETS - 1)
    return jnp.where(n < max_exact, n, large)


def _bucket_tiles():
    T = ATT_TQ
    kpos = jnp.arange(T)[:, None]
    qpos = jnp.arange(T)[None, :]
    rel0 = qpos - kpos
    rel1 = rel0 + T
    t0 = jnp.where(rel0 >= 0, _t5_bucket(rel0), -1)
    return jnp.stack([t0, _t5_bucket(rel1)]).astype(jnp.int32)


def kernel(x, c, w_ada, b_ada, norm_ffn1, w_ffn1_in, w_ffn1_out, norm_mix, w_in, lambda_q1, lambda_k1,
           lambda_q2, lambda_k2, subln_gain, group_scale, w_out, norm_ffn2, w_ffn2_in, w_ffn2_out, rel_bias,
           norm_final):
    B, S, D = x.shape
    l = 0
    mod = _mod_call(c, w_ada[l], b_ada[l]).reshape(B, N_MOD, D)

    x = _ffn_call(x, mod, norm_ffn1[l], w_ffn1_in[l].astype(BF16), w_ffn1_out[l].astype(BF16), mod_row=0)

    cos_tab, sin_tab = _rope_tables(S)
    proj = _inproj_call(x, mod, norm_mix[l], w_in[l].astype(BF16), cos_tab, sin_tab)
    y_ret = _ret_call(proj, *_retention_tables())
    bias_tiles = _bias_call(rel_bias, _bucket_tiles())
    y_diff = _attn_call(proj, bias_tiles, lambda_q1[l][None], lambda_k1[l][None], lambda_q2[l][None],
                        lambda_k2[l][None], subln_gain[l])

    mixer = (y_ret, y_diff, group_scale[l], w_out[l].astype(BF16), norm_final)
    return _ffn_call(x, mod, norm_ffn2[l], w_ffn2_in[l].astype(BF16), w_ffn2_out[l].astype(BF16), mixer,
                     mod_row=6)
```

```python
import functools
import math

import jax
import jax.numpy as jnp
import numpy as np
from jax import lax
from jax.experimental import pallas as pl
from jax.experimental.pallas import tpu as pltpu

D_MODEL = 1024
D_FF = 2816
EPS = 1e-6
RET_DIM = 64
RET_WIDTH = 512
RET_HEADS = 8
DIFF_DIM = 64
DIFF_WIDTH = 512
DIFF_HEADS = 4
IN_WIDTH = 4 * RET_WIDTH + 3 * DIFF_WIDTH
N_BUCKETS = 32
MAX_DIST = 128
CHUNK = 128
ROPE_BASE = 10000.0
N_MOD = 9
NEG_INF = -1e30
LAMBDA_INIT = 0.8 - 0.6 * math.exp(-0.3 * 0)

V7X_LANES = 128
V7X_MXU_DIM = 256
V7X_VMEM_BYTES = 64 * 1024 * 1024

FFN_TM = 1024
FFN_TS = 256
FFN_TF = V7X_MXU_DIM
PROJ_TM = 512
ATT_TQ = 512
ATT_TK = 1024
ATT_UNROLL = 2
ATT_ONES_ROWS = 16
RET_UNROLL = 8
LN_ROWS = 1024
LOG2E = math.log2(math.e)
SEG = 512

BF16 = jnp.bfloat16
F32 = jnp.float32


def _vmem_limit(nbytes):
    return int(min(nbytes + (12 << 20), V7X_VMEM_BYTES - (4 << 20)))


def _dot(a, b):
    return jnp.dot(a, b, preferred_element_type=F32)


def _dot_nt(a, b):
    return lax.dot_general(a, b, (((1,), (1,)), ((), ())), preferred_element_type=F32)


def _silu(x):
    return x * (1.0 / (1.0 + jnp.exp(-x)))


def _rms_mod(x, g, shift, scale):
    y = x * lax.rsqrt(jnp.mean(x * x, axis=-1, keepdims=True) + EPS)
    return (y * g) * (1.0 + scale) + shift


def _mod_kernel(c_ref, w_ref, b_ref, o_ref):
    c = c_ref[...]
    o_ref[...] = _dot(_silu(c).astype(BF16), w_ref[...].astype(BF16)) + b_ref[...]


def _mod_call(c, w_ada, b_ada):
    B = c.shape[0]
    n = N_MOD * D_MODEL
    tn = D_MODEL
    return pl.pallas_call(
        _mod_kernel,
        grid=(n // tn,),
        in_specs=[
            pl.BlockSpec((B, D_MODEL), lambda j: (0, 0)),
            pl.BlockSpec((D_MODEL, tn), lambda j: (0, j)),
            pl.BlockSpec((1, tn), lambda j: (0, j)),
        ],
        out_specs=pl.BlockSpec((B, tn), lambda j: (0, j)),
        out_shape=jax.ShapeDtypeStruct((B, n), F32),
        compiler_params=pltpu.CompilerParams(
            dimension_semantics=("parallel",),
            vmem_limit_bytes=_vmem_limit(2 * D_MODEL * tn * 4)),
        name="mod",
    )(c, w_ada, b_ada.reshape(1, n))


def _ffn_kernel(*refs, mod_row, mix):
    if mix:
        (x_ref, mod_ref, g_ref, win_ref, wout_ref, yr_ref, yd_ref, gs_ref, wmix_ref, gf_ref,
         o_ref, h_ref, act_ref) = refs
    else:
        x_ref, mod_ref, g_ref, win_ref, wout_ref, o_ref, h_ref, act_ref = refs
    shift = mod_ref[0, mod_row:mod_row + 1, :]
    scale = mod_ref[0, mod_row + 1:mod_row + 2, :]
    gate = mod_ref[0, mod_row + 2:mod_row + 3, :]
    n_sub = FFN_TM // FFN_TS

    def prologue(sub):
        rows = slice(sub * FFN_TS, (sub + 1) * FFN_TS)
        x = x_ref[0, rows, :]
        if mix:
            gs = gs_ref[...]
            ya = (yr_ref[0, rows, :].astype(F32) * gs[:, :RET_WIDTH]).astype(BF16)
            yb = (yd_ref[0, rows, :].astype(F32) * gs[:, RET_WIDTH:]).astype(BF16)
            x = x + mod_ref[0, 5:6, :] * (_dot(ya, wmix_ref[:RET_WIDTH, :]) + _dot(yb, wmix_ref[RET_WIDTH:, :]))
        h_ref[rows, :] = _rms_mod(x, g_ref[...], shift, scale).astype(BF16)
        return x

    x_next = prologue(0)
    for sub in range(n_sub):
        rows = slice(sub * FFN_TS, (sub + 1) * FFN_TS)
        x = x_next
        if sub + 1 < n_sub:
            x_next = prologue(sub + 1)
        for j in range(D_FF // FFN_TF):
            lo = j * FFN_TF
            g = _dot(h_ref[rows, :], win_ref[:, lo:lo + FFN_TF])
            u = _dot(h_ref[rows, :], win_ref[:, D_FF + lo:D_FF + lo + FFN_TF])
            act_ref[rows, lo:lo + FFN_TF] = (_silu(g) * u).astype(BF16)
        y = x + (0.5 * gate) * _dot(act_ref[rows, :], wout_ref[...])
        if mix:
            y = (y * lax.rsqrt(jnp.mean(y * y, axis=-1, keepdims=True) + EPS)) * gf_ref[...]
        o_ref[0, rows, :] = y


def _ffn_call(x, mod, norm_g, w_in, w_out, mixer=None, *, mod_row):
    B, S, D = x.shape
    tm = FFN_TM
    const = lambda b, i: (0, 0)
    rows = lambda width: pl.BlockSpec((1, tm, width), lambda b, i: (b, i, 0))
    resident = lambda shape: pl.BlockSpec(shape, const, pipeline_mode=pl.Buffered(1))
    in_specs = [
        rows(D),
        pl.BlockSpec((1, N_MOD, D), lambda b, i: (b, 0, 0)),
        pl.BlockSpec((1, D), const),
        resident((D, 2 * D_FF)),
        resident((D_FF, D)),
    ]
    args = [x, mod, norm_g.reshape(1, D), w_in, w_out]
    nbytes = 4 * tm * D * 4 + (D * 2 * D_FF + D_FF * D) * 2 + tm * D * 2 + tm * D_FF * 2
    if mixer is not None:
        y_ret, y_diff, group_scale, w_mix, norm_final = mixer
        in_specs += [rows(RET_WIDTH), rows(DIFF_WIDTH), pl.BlockSpec((1, D), const), resident((D, D)),
                     pl.BlockSpec((1, D), const)]
        args += [y_ret, y_diff, group_scale.reshape(1, D), w_mix, norm_final.reshape(1, D)]
        nbytes += 4 * tm * RET_WIDTH * 2 + D * D * 2
    return pl.pallas_call(
        functools.partial(_ffn_kernel, mod_row=mod_row, mix=mixer is not None),
        grid=(B, S // tm),
        in_specs=in_specs,
        out_specs=rows(D),
        out_shape=jax.ShapeDtypeStruct((B, S, D), F32),
        scratch_shapes=[pltpu.VMEM((tm, D), BF16), pltpu.VMEM((tm, D_FF), BF16)],
        compiler_params=pltpu.CompilerParams(
            dimension_semantics=("parallel", "parallel"),
            vmem_limit_bytes=_vmem_limit(nbytes)),
        name="mix_ffn_final" if mixer is not None else "ffn",
    )(*args)


def _inproj_kernel(x_ref, mod_ref, g_ref, w_ref, cos_ref, sin_ref, o_ref, h_ref):
    x = x_ref[0]
    h_ref[...] = _rms_mod(x, g_ref[...], mod_ref[0, 3:4, :], mod_ref[0, 4:5, :]).astype(BF16)
    lane = lax.broadcasted_iota(jnp.int32, (1, V7X_LANES), 1)
    first_half = (lane % RET_DIM) < (RET_DIM // 2)
    cos = cos_ref[...]
    sin = sin_ref[...]
    for seg in range(IN_WIDTH // SEG):
        p = _dot(h_ref[...], w_ref[:, seg * SEG:(seg + 1) * SEG])
        if seg in (0, 1):
            post = RET_DIM ** -0.5 if seg == 1 else 1.0
            for c in range(SEG // V7X_LANES):
                v = p[:, c * V7X_LANES:(c + 1) * V7X_LANES]
                rot = jnp.where(first_half,
                                pltpu.roll(v, V7X_LANES - RET_DIM // 2, axis=1),
                                pltpu.roll(v, RET_DIM // 2, axis=1))
                r = v * cos + rot * sin
                if post != 1.0:
                    r = r * post
                o_ref[0, :, seg * SEG + c * V7X_LANES:seg * SEG + (c + 1) * V7X_LANES] = r.astype(BF16)
        elif seg == 3:
            o_ref[0, :, seg * SEG:(seg + 1) * SEG] = _silu(p).astype(BF16)
        elif seg == 4:
            o_ref[0, :, seg * SEG:(seg + 1) * SEG] = (p * (LOG2E * DIFF_DIM ** -0.5)).astype(BF16)
        else:
            o_ref[0, :, seg * SEG:(seg + 1) * SEG] = p.astype(BF16)


def _inproj_call(x, mod, norm_g, w_in, cos_tab, sin_tab):
    B, S, D = x.shape
    tm = PROJ_TM
    const = lambda b, i: (0, 0)
    nbytes = 2 * tm * D * 4 + 2 * D * IN_WIDTH * 2 + 2 * tm * IN_WIDTH * 2 + tm * D * 2 + 4 * tm * V7X_LANES * 4
    return pl.pallas_call(
        _inproj_kernel,
        grid=(B, S // tm),
        in_specs=[
            pl.BlockSpec((1, tm, D), lambda b, i: (b, i, 0)),
            pl.BlockSpec((1, N_MOD, D), lambda b, i: (b, 0, 0)),
            pl.BlockSpec((1, D), const),
            pl.BlockSpec((D, IN_WIDTH), const),
            pl.BlockSpec((tm, V7X_LANES), lambda b, i: (i, 0)),
            pl.BlockSpec((tm, V7X_LANES), lambda b, i: (i, 0)),
        ],
        out_specs=pl.BlockSpec((1, tm, IN_WIDTH), lambda b, i: (b, i, 0)),
        out_shape=jax.ShapeDtypeStruct((B, S, IN_WIDTH), BF16),
        scratch_shapes=[pltpu.VMEM((tm, D), BF16)],
        compiler_params=pltpu.CompilerParams(
            dimension_semantics=("parallel", "parallel"),
            vmem_limit_bytes=_vmem_limit(nbytes)),
        name="inproj",
    )(x, mod, norm_g.reshape(1, D), w_in, cos_tab, sin_tab)


ATT_N_TILES = ATT_TK // ATT_TQ + 2


def _build_bias_tiles(rb_ref, bk_ref, bias_ref, h):
    sb = V7X_LANES
    R = ATT_TK // ATT_TQ
    nk, nq = ATT_TK // sb, ATT_TQ // sb
    far = rb_ref[N_BUCKETS - 1, h]
    near = []
    for t in range(2):
        bk = bk_ref[t]
        tile = jnp.zeros(bk.shape, F32)
        for b in range(N_BUCKETS - 1):
            tile = jnp.where(bk == b, (rb_ref[b, h] - far) * LOG2E, tile)
        near.append(jnp.where(bk < 0, NEG_INF, tile))
    zeros = jnp.zeros((sb, sb), F32)
    masked = jnp.full((sb, sb), NEG_INF, F32)
    for t in range(ATT_N_TILES):
        for ka in range(nk):
            for qa in range(nq):
                d = t * nq + qa - ka if t < R else nk + qa - ka if t == R else 2
                blk = masked if d < 0 else near[d] if d < 2 else zeros
                bias_ref[t, ka * sb:(ka + 1) * sb, qa * sb:(qa + 1) * sb] = blk


def _ret_kernel(q_ref, k_ref, v_ref, gate_ref, dec_ref, zeta_ref, xi_ref, gc_ref, o_ref, kv_ref, r_ref, y_ref):
    C = CHUNK
    U = RET_UNROLL
    S = q_ref.shape[1]
    n_chunks = S // C
    lane = lax.broadcasted_iota(jnp.int32, (1, V7X_LANES), 1)
    m_a = jnp.where(lane < RET_DIM, 1.0, 0.0).astype(BF16)
    m_b = jnp.where(lane >= RET_DIM, 1.0, 0.0).astype(BF16)
    row = lax.broadcasted_iota(jnp.int32, (V7X_LANES, V7X_LANES), 0)
    col = lax.broadcasted_iota(jnp.int32, (V7X_LANES, V7X_LANES), 1)
    same_head = jnp.where((row < RET_DIM) == (col < RET_DIM), 1.0, 0.0)
    seg_mean = (same_head * (1.0 / RET_DIM)).astype(BF16)
    gc = gc_ref[0]

    def kv_group(g, carry):
        for u in range(U):
            n = g * U + u
            sl = pl.ds(pl.multiple_of(n * C, C), C)
            kz = (k_ref[0, sl, :].astype(F32) * zeta_ref[0]).T.astype(BF16)
            kv_ref[n] = _dot(kz, v_ref[0, sl, :]) * same_head
        return carry

    lax.fori_loop(0, n_chunks // U, kv_group, 0)

    def scan_step(n, r):
        r_ref[n] = r.astype(BF16)
        return r * gc + kv_ref[n]

    lax.fori_loop(0, n_chunks, scan_step, jnp.zeros((V7X_LANES, V7X_LANES), F32))

    def out_group(g, carry):
        for u in range(U):
            n = g * U + u
            sl = pl.ds(pl.multiple_of(n * C, C), C)
            q = q_ref[0, sl, :]
            k = k_ref[0, sl, :]
            v = v_ref[0, sl, :]
            kk = jnp.concatenate([k * m_a, k * m_b], axis=0)
            vv = jnp.concatenate([v * m_a, v * m_b], axis=0)
            s = _dot_nt(q, kk) * dec_ref[0]
            y_ref[sl, :] = _dot(s.astype(BF16), vv) + _dot(q, r_ref[n]) * xi_ref[0]
        return carry

    lax.fori_loop(0, n_chunks // U, out_group, 0)

    def ln_block(i, carry):
        sl = pl.ds(pl.multiple_of(i * LN_ROWS, LN_ROWS), LN_ROWS)
        y = y_ref[sl, :]
        d = y - _dot(y.astype(BF16), seg_mean)
        var = _dot((d * d).astype(BF16), seg_mean)
        o_ref[0, sl, :] = ((d * lax.rsqrt(var + EPS)) * gate_ref[0, sl, :].astype(F32)).astype(BF16)
        return carry

    lax.fori_loop(0, S // LN_ROWS, ln_block, 0)


def _ret_call(proj, dec, zeta, xi, gc):
    B, S, _ = proj.shape
    n_pairs = RET_HEADS // 2
    blk = lambda off: pl.BlockSpec((1, S, V7X_LANES), lambda b, p: (b, 0, off + p))
    pair = lambda shape: pl.BlockSpec((1,) + shape, lambda b, p: (p, 0, 0))
    return pl.pallas_call(
        _ret_kernel,
        grid=(B, n_pairs),
        in_specs=[
            blk(0), blk(n_pairs), blk(2 * n_pairs), blk(3 * n_pairs),
            pair((CHUNK, 2 * CHUNK)), pair((CHUNK, V7X_LANES)), pair((CHUNK, V7X_LANES)),
            pair((V7X_LANES, V7X_LANES)),
        ],
        out_specs=pl.BlockSpec((1, S, V7X_LANES), lambda b, p: (b, 0, p)),
        out_shape=jax.ShapeDtypeStruct((B, S, RET_WIDTH), BF16),
        scratch_shapes=[
            pltpu.VMEM((S // CHUNK, V7X_LANES, V7X_LANES), F32),
            pltpu.VMEM((S // CHUNK, V7X_LANES, V7X_LANES), BF16),
            pltpu.VMEM((S, V7X_LANES), F32),
        ],
        compiler_params=pltpu.CompilerParams(dimension_semantics=("parallel", "parallel")),
        name="retention",
    )(proj, proj, proj, proj, dec, zeta, xi, gc)


def _attn_steps(S):
    R = ATT_TK // ATT_TQ
    steps = []
    for qi in range(S // ATT_TQ):
        c_diag = qi // R
        for c in range(c_diag + 1):
            if c == c_diag:
                tile = qi % R
            elif c == c_diag - 1 and qi % R == 0:
                tile = R
            else:
                tile = R + 1
            steps.append((qi, c, tile, int(c == 0)))
    return steps


def _attn_kernel(steps_ref, rb_ref, q_ref, k_ref, v_ref, bk_ref, lq1_ref, lk1_ref, lq2_ref, lk2_ref, g_ref, o_ref,
                 bias_ref, vt_ref, sa_ref, sb_ref, ca_ref, cb_ref, m_ref, acc_ref, *, n_steps):
    TQ, TK = ATT_TQ, ATT_TK
    DV = 2 * DIFF_DIM
    S = q_ref.shape[1]
    lane = lax.broadcasted_iota(jnp.int32, (1, V7X_LANES), 1)
    masks = (jnp.where(lane < DIFF_DIM, 1.0, 0.0).astype(BF16),
             jnp.where(lane >= DIFF_DIM, 1.0, 0.0).astype(BF16))

    @pl.when(pl.program_id(1) == 0)
    def _():
        _build_bias_tiles(rb_ref, bk_ref, bias_ref, pl.program_id(0))

    def transpose_v(c, carry):
        sl = pl.ds(pl.multiple_of(c * TK, TK), TK)
        vt_ref[c, :DV, :] = v_ref[0, sl, :].astype(F32).T.astype(BF16)
        vt_ref[c, DV:, :] = jnp.ones((ATT_ONES_ROWS, TK), BF16)
        return carry

    lax.fori_loop(0, S // TK, transpose_v, 0)

    @pl.when(jnp.logical_and(pl.program_id(0) == 0, pl.program_id(1) == 0))
    def _():
        m_ref[...] = jnp.zeros(m_ref.shape, F32)
        acc_ref[...] = jnp.zeros(acc_ref.shape, F32)

    def score(t, m, s_ref, c_ref):
        qi = steps_ref[0, t]
        c = steps_ref[1, t]
        q = q_ref[0, pl.ds(pl.multiple_of(qi * TQ, TQ), TQ), :]
        k = k_ref[0, pl.ds(pl.multiple_of(c * TK, TK), TK), :]
        s = _dot_nt(k, q * masks[m]) + bias_ref[steps_ref[2, t]]
        s_ref[m] = s
        c_ref[m] = jnp.max(s, axis=0, keepdims=True)

    def softmax_part(t, m, s_ref, c_ref):
        qi = steps_ref[0, t]
        restart = steps_ref[3, t] == 1
        m_old = jnp.where(restart, NEG_INF, m_ref[qi, m])
        m_new = jnp.maximum(m_old, c_ref[m])
        m_ref[qi, m] = m_new
        return jnp.exp2(s_ref[m] - m_new).astype(BF16), jnp.exp2(m_old - m_new)

    def value_part(t, m, p, alpha):
        qi = steps_ref[0, t]
        restart = steps_ref[3, t] == 1
        pv = _dot(vt_ref[steps_ref[1, t]], p)
        acc_ref[qi, m] = jnp.where(restart, 0.0, alpha * acc_ref[qi, m]) + pv

    def step(t, cur, nxt):
        score(t + 1, 0, *nxt)
        p0, a0 = softmax_part(t, 0, *cur)
        score(t + 1, 1, *nxt)
        value_part(t, 0, p0, a0)
        p1, a1 = softmax_part(t, 1, *cur)
        value_part(t, 1, p1, a1)

    buf_a = (sa_ref, ca_ref)
    buf_b = (sb_ref, cb_ref)
    score(0, 0, *buf_a)
    score(0, 1, *buf_a)

    U = ATT_UNROLL
    def step_group(i, carry):
        for u in range(0, U, 2):
            step(U * i + u, buf_a, buf_b)
            step(U * i + u + 1, buf_b, buf_a)
        return carry

    lax.fori_loop(0, n_steps // U, step_group, 0)
    for t in range(n_steps - n_steps % U, n_steps):
        step(t, *((buf_a, buf_b) if t % 2 == 0 else (buf_b, buf_a)))

    lam = (jnp.exp(jnp.sum(lq1_ref[...] * lk1_ref[...], axis=-1, keepdims=True))
           - jnp.exp(jnp.sum(lq2_ref[...] * lk2_ref[...], axis=-1, keepdims=True)) + LAMBDA_INIT)

    def finalize(qi, carry):
        a0 = acc_ref[qi, 0]
        a1 = acc_ref[qi, 1]
        o = a0[:DV] / a0[DV:DV + 1] - lam * (a1[:DV] / a1[DV:DV + 1])
        y = o * lax.rsqrt(jnp.mean(o * o, axis=0, keepdims=True) + EPS)
        y = (y * g_ref[...]) * (1.0 - LAMBDA_INIT)
        o_ref[0, pl.ds(pl.multiple_of(qi * TQ, TQ), TQ), :] = y.T.astype(BF16)
        return carry

    lax.fori_loop(0, S // TQ, finalize, 0)


def _attn_call(proj, rel_bias, buckets, lq1, lk1, lq2, lk2, subln_g):
    B, S, _ = proj.shape
    TQ, TK = ATT_TQ, ATT_TK
    n_tiles = ATT_N_TILES
    n_q = S // TQ
    steps = _attn_steps(S)
    table = jnp.asarray(np.array(steps + [steps[-1]], np.int32).T)
    q_off = 4 * RET_WIDTH // V7X_LANES
    rows = 2 * DIFF_DIM + ATT_ONES_ROWS
    nbytes = (8 * S * V7X_LANES * 2 + n_tiles * TK * TQ * 4 + S * rows * 2
              + n_q * 2 * rows * TQ * 4 + 4 * TK * TQ * 4 + 3 * TK * TQ * 4)
    blk = lambda off: pl.BlockSpec((1, S, V7X_LANES), lambda h, b: (b, 0, off + h))
    vec = pl.BlockSpec((1, DIFF_DIM), lambda h, b: (0, 0))
    return pl.pallas_call(
        functools.partial(_attn_kernel, n_steps=len(steps)),
        grid=(DIFF_HEADS, B),
        in_specs=[
            pl.BlockSpec(memory_space=pltpu.SMEM),
            pl.BlockSpec(memory_space=pltpu.SMEM),
            blk(q_off), blk(q_off + DIFF_HEADS), blk(q_off + 2 * DIFF_HEADS),
            pl.BlockSpec((2, V7X_LANES, V7X_LANES), lambda h, b: (0, 0, 0)),
            vec, vec, vec, vec,
            pl.BlockSpec((2 * DIFF_DIM, 1), lambda h, b: (0, 0)),
        ],
        out_specs=pl.BlockSpec((1, S, V7X_LANES), lambda h, b: (b, 0, h)),
        out_shape=jax.ShapeDtypeStruct((B, S, DIFF_WIDTH), BF16),
        scratch_shapes=[
            pltpu.VMEM((n_tiles, TK, TQ), F32),
            pltpu.VMEM((S // TK, rows, TK), BF16),
            pltpu.VMEM((2, TK, TQ), F32),
            pltpu.VMEM((2, TK, TQ), F32),
            pltpu.VMEM((2, 1, TQ), F32),
            pltpu.VMEM((2, 1, TQ), F32),
            pltpu.VMEM((n_q, 2, 1, TQ), F32),
            pltpu.VMEM((n_q, 2, rows, TQ), F32),
        ],
        compiler_params=pltpu.CompilerParams(
            dimension_semantics=("arbitrary", "arbitrary"),
            vmem_limit_bytes=_vmem_limit(nbytes)),
        name="diffattn",
    )(table, rel_bias, proj, proj, proj, buckets, lq1, lk1, lq2, lk2, subln_g.reshape(2 * DIFF_DIM, 1))


def _rope_tables(S):
    half = RET_DIM // 2
    inv = ROPE_BASE ** (-jnp.arange(0, RET_DIM, 2, dtype=F32) / RET_DIM)
    ang = jnp.arange(S).astype(F32)[:, None] * inv[None, :]
    cos = jnp.cos(ang)
    sin = jnp.sin(ang)
    reps = V7X_LANES // RET_DIM
    cos_tab = jnp.tile(jnp.concatenate([cos, cos], axis=1), (1, reps))
    sin_tab = jnp.tile(jnp.concatenate([-sin, sin], axis=1), (1, reps))
    del half
    return cos_tab, sin_tab


def _retention_tables():
    C = CHUNK
    log_gamma = jnp.log1p(-(2.0 ** (-5.0 - jnp.arange(RET_HEADS, dtype=F32))))
    idx = jnp.arange(C, dtype=F32)
    dist = idx[:, None] - idx[None, :]
    decay = jnp.where(dist >= 0, jnp.exp(log_gamma[:, None, None] * jnp.maximum(dist, 0.0)[None]), 0.0)
    zeta = jnp.exp(log_gamma[:, None] * (C - 1 - idx)[None])
    xi = jnp.exp(log_gamma[:, None] * (idx + 1.0)[None])
    gamma_c = jnp.exp(log_gamma * C)
    n_pairs = RET_HEADS // 2
    dec = decay.reshape(n_pairs, 2, C, C).transpose(0, 2, 1, 3).reshape(n_pairs, C, 2 * C)
    lanes = lambda t: jnp.repeat(t.reshape(n_pairs, 2, C).transpose(0, 2, 1), RET_DIM, axis=2)
    g_rows = jnp.repeat(gamma_c.reshape(n_pairs, 2), RET_DIM, axis=1)
    blockdiag = (jnp.arange(V7X_LANES)[:, None] < RET_DIM) == (jnp.arange(V7X_LANES)[None, :] < RET_DIM)
    gc = jnp.where(blockdiag[None], g_rows[:, :, None], 0.0)
    return dec, lanes(zeta), lanes(xi), gc


def _t5_bucket(rel):
    n = jnp.maximum(rel, 0)
    max_exact = N_BUCKETS // 2
    nf = jnp.maximum(n, 1).astype(F32)
    large = max_exact + (jnp.log(nf / max_exact) / math.log(MAX_DIST / max_exact)
                         * (N_BUCKETS - max_exact)).astype(jnp.int32)
    large = jnp.minimum(large, N_BUCKETS - 1)
    return jnp.where(n < max_exact, n, large)


def _bucket_tiles():
    T = V7X_LANES
    kpos = jnp.arange(T)[:, None]
    qpos = jnp.arange(T)[None, :]
    rel0 = qpos - kpos
    rel1 = rel0 + T
    t0 = jnp.where(rel0 >= 0, _t5_bucket(rel0), -1)
    return jnp.stack([t0, _t5_bucket(rel1)]).astype(jnp.int32)


def kernel(x, c, w_ada, b_ada, norm_ffn1, w_ffn1_in, w_ffn1_out, norm_mix, w_in, lambda_q1, lambda_k1,
           lambda_q2, lambda_k2, subln_gain, group_scale, w_out, norm_ffn2, w_ffn2_in, w_ffn2_out, rel_bias,
           norm_final):
    B, S, D = x.shape
    l = 0
    mod = _mod_call(c, w_ada[l], b_ada[l]).reshape(B, N_MOD, D)

    x = _ffn_call(x, mod, norm_ffn1[l], w_ffn1_in[l].astype(BF16), w_ffn1_out[l].astype(BF16), mod_row=0)

    cos_tab, sin_tab = _rope_tables(S)
    proj = _inproj_call(x, mod, norm_mix[l], w_in[l].astype(BF16), cos_tab, sin_tab)
    y_ret = _ret_call(proj, *_retention_tables())
    y_diff = _attn_call(proj, rel_bias, _bucket_tiles(), lambda_q1[l][None], lambda_k1[l][None],
                        lambda_q2[l][None], lambda_k2[l][None], subln_gain[l])

    mixer = (y_ret, y_diff, group_scale[l], w_out[l].astype(BF16), norm_final)
    return _ffn_call(x, mod, norm_ffn2[l], w_ffn2_in[l].astype(BF16), w_ffn2_out[l].astype(BF16), mixer,
                     mod_row=6)
```

```python
import functools
import math

import jax
import jax.numpy as jnp
import numpy as np
from jax import lax
from jax.experimental import pallas as pl
from jax.experimental.pallas import tpu as pltpu

D_MODEL = 1024
D_FF = 2816
EPS = 1e-6
RET_DIM = 64
RET_WIDTH = 512
RET_HEADS = 8
DIFF_DIM = 64
DIFF_WIDTH = 512
DIFF_HEADS = 4
IN_WIDTH = 4 * RET_WIDTH + 3 * DIFF_WIDTH
N_BUCKETS = 32
MAX_DIST = 128
CHUNK = 128
ROPE_BASE = 10000.0
N_MOD = 9
NEG_INF = -1e30
LAMBDA_INIT = 0.8 - 0.6 * math.exp(-0.3 * 0)

V7X_LANES = 128
V7X_MXU_DIM = 256
V7X_VMEM_BYTES = 64 * 1024 * 1024

FFN_TM = 1024
FFN_TS = 256
FFN_TF = V7X_MXU_DIM
PROJ_TM = 512
ATT_TQ = 512
ATT_TK = 1024
ATT_ONES_ROWS = 16
RET_UNROLL = 8
LN_ROWS = 1024
LOG2E = math.log2(math.e)
SEG = 512

BF16 = jnp.bfloat16
F32 = jnp.float32


def _vmem_limit(nbytes):
    return int(min(nbytes + (12 << 20), V7X_VMEM_BYTES - (4 << 20)))


def _dot(a, b):
    return jnp.dot(a, b, preferred_element_type=F32)


def _dot_nt(a, b):
    return lax.dot_general(a, b, (((1,), (1,)), ((), ())), preferred_element_type=F32)


def _silu(x):
    return x * (1.0 / (1.0 + jnp.exp(-x)))


def _rms_mod(x, g, shift, scale):
    y = x * lax.rsqrt(jnp.mean(x * x, axis=-1, keepdims=True) + EPS)
    return (y * g) * (1.0 + scale) + shift


def _mod_kernel(c_ref, w_ref, b_ref, o_ref):
    c = c_ref[...]
    o_ref[...] = _dot(_silu(c).astype(BF16), w_ref[...].astype(BF16)) + b_ref[...]


def _mod_call(c, w_ada, b_ada):
    B = c.shape[0]
    n = N_MOD * D_MODEL
    tn = D_MODEL
    return pl.pallas_call(
        _mod_kernel,
        grid=(n // tn,),
        in_specs=[
            pl.BlockSpec((B, D_MODEL), lambda j: (0, 0)),
            pl.BlockSpec((D_MODEL, tn), lambda j: (0, j)),
            pl.BlockSpec((1, tn), lambda j: (0, j)),
        ],
        out_specs=pl.BlockSpec((B, tn), lambda j: (0, j)),
        out_shape=jax.ShapeDtypeStruct((B, n), F32),
        compiler_params=pltpu.CompilerParams(
            dimension_semantics=("parallel",),
            vmem_limit_bytes=_vmem_limit(2 * D_MODEL * tn * 4)),
        name="mod",
    )(c, w_ada, b_ada.reshape(1, n))


def _ffn_kernel(*refs, mod_row, mix):
    if mix:
        (x_ref, mod_ref, g_ref, win_ref, wout_ref, yr_ref, yd_ref, gs_ref, wmix_ref, gf_ref,
         o_ref, h_ref, act_ref) = refs
    else:
        x_ref, mod_ref, g_ref, win_ref, wout_ref, o_ref, h_ref, act_ref = refs
    shift = mod_ref[0, mod_row:mod_row + 1, :]
    scale = mod_ref[0, mod_row + 1:mod_row + 2, :]
    gate = mod_ref[0, mod_row + 2:mod_row + 3, :]
    n_sub = FFN_TM // FFN_TS

    def prologue(sub):
        rows = slice(sub * FFN_TS, (sub + 1) * FFN_TS)
        x = x_ref[0, rows, :]
        if mix:
            gs = gs_ref[...]
            ya = (yr_ref[0, rows, :].astype(F32) * gs[:, :RET_WIDTH]).astype(BF16)
            yb = (yd_ref[0, rows, :].astype(F32) * gs[:, RET_WIDTH:]).astype(BF16)
            x = x + mod_ref[0, 5:6, :] * (_dot(ya, wmix_ref[:RET_WIDTH, :]) + _dot(yb, wmix_ref[RET_WIDTH:, :]))
        h_ref[rows, :] = _rms_mod(x, g_ref[...], shift, scale).astype(BF16)
        return x

    x_next = prologue(0)
    for sub in range(n_sub):
        rows = slice(sub * FFN_TS, (sub + 1) * FFN_TS)
        x = x_next
        if sub + 1 < n_sub:
            x_next = prologue(sub + 1)
        for j in range(D_FF // FFN_TF):
            lo = j * FFN_TF
            g = _dot(h_ref[rows, :], win_ref[:, lo:lo + FFN_TF])
            u = _dot(h_ref[rows, :], win_ref[:, D_FF + lo:D_FF + lo + FFN_TF])
            act_ref[rows, lo:lo + FFN_TF] = (_silu(g) * u).astype(BF16)
        y = x + (0.5 * gate) * _dot(act_ref[rows, :], wout_ref[...])
        if mix:
            y = (y * lax.rsqrt(jnp.mean(y * y, axis=-1, keepdims=True) + EPS)) * gf_ref[...]
        o_ref[0, rows, :] = y


def _ffn_call(x, mod, norm_g, w_in, w_out, mixer=None, *, mod_row):
    B, S, D = x.shape
    tm = FFN_TM
    const = lambda b, i: (0, 0)
    rows = lambda width: pl.BlockSpec((1, tm, width), lambda b, i: (b, i, 0))
    resident = lambda shape: pl.BlockSpec(shape, const, pipeline_mode=pl.Buffered(1))
    in_specs = [
        rows(D),
        pl.BlockSpec((1, N_MOD, D), lambda b, i: (b, 0, 0)),
        pl.BlockSpec((1, D), const),
        resident((D, 2 * D_FF)),
        resident((D_FF, D)),
    ]
    args = [x, mod, norm_g.reshape(1, D), w_in, w_out]
    nbytes = 4 * tm * D * 4 + (D * 2 * D_FF + D_FF * D) * 2 + tm * D * 2 + tm * D_FF * 2
    if mixer is not None:
        y_ret, y_diff, group_scale, w_mix, norm_final = mixer
        in_specs += [rows(RET_WIDTH), rows(DIFF_WIDTH), pl.BlockSpec((1, D), const), resident((D, D)),
                     pl.BlockSpec((1, D), const)]
        args += [y_ret, y_diff, group_scale.reshape(1, D), w_mix, norm_final.reshape(1, D)]
        nbytes += 4 * tm * RET_WIDTH * 2 + D * D * 2
    return pl.pallas_call(
        functools.partial(_ffn_kernel, mod_row=mod_row, mix=mixer is not None),
        grid=(B, S // tm),
        in_specs=in_specs,
        out_specs=rows(D),
        out_shape=jax.ShapeDtypeStruct((B, S, D), F32),
        scratch_shapes=[pltpu.VMEM((tm, D), BF16), pltpu.VMEM((tm, D_FF), BF16)],
        compiler_params=pltpu.CompilerParams(
            dimension_semantics=("parallel", "parallel"),
            vmem_limit_bytes=_vmem_limit(nbytes)),
        name="mix_ffn_final" if mixer is not None else "ffn",
    )(*args)


def _inproj_kernel(x_ref, mod_ref, g_ref, w_ref, cos_ref, sin_ref, o_ref, h_ref):
    x = x_ref[0]
    h_ref[...] = _rms_mod(x, g_ref[...], mod_ref[0, 3:4, :], mod_ref[0, 4:5, :]).astype(BF16)
    lane = lax.broadcasted_iota(jnp.int32, (1, V7X_LANES), 1)
    first_half = (lane % RET_DIM) < (RET_DIM // 2)
    cos = cos_ref[...]
    sin = sin_ref[...]
    for seg in range(IN_WIDTH // SEG):
        p = _dot(h_ref[...], w_ref[:, seg * SEG:(seg + 1) * SEG])
        if seg in (0, 1):
            post = RET_DIM ** -0.5 if seg == 1 else 1.0
            for c in range(SEG // V7X_LANES):
                v = p[:, c * V7X_LANES:(c + 1) * V7X_LANES]
                rot = jnp.where(first_half,
                                pltpu.roll(v, V7X_LANES - RET_DIM // 2, axis=1),
                                pltpu.roll(v, RET_DIM // 2, axis=1))
                r = v * cos + rot * sin
                if post != 1.0:
                    r = r * post
                o_ref[0, :, seg * SEG + c * V7X_LANES:seg * SEG + (c + 1) * V7X_LANES] = r.astype(BF16)
        elif seg == 3:
            o_ref[0, :, seg * SEG:(seg + 1) * SEG] = _silu(p).astype(BF16)
        elif seg == 4:
            o_ref[0, :, seg * SEG:(seg + 1) * SEG] = (p * (LOG2E * DIFF_DIM ** -0.5)).astype(BF16)
        else:
            o_ref[0, :, seg * SEG:(seg + 1) * SEG] = p.astype(BF16)


def _inproj_call(x, mod, norm_g, w_in, cos_tab, sin_tab):
    B, S, D = x.shape
    tm = PROJ_TM
    const = lambda b, i: (0, 0)
    nbytes = 2 * tm * D * 4 + 2 * D * IN_WIDTH * 2 + 2 * tm * IN_WIDTH * 2 + tm * D * 2 + 4 * tm * V7X_LANES * 4
    return pl.pallas_call(
        _inproj_kernel,
        grid=(B, S // tm),
        in_specs=[
            pl.BlockSpec((1, tm, D), lambda b, i: (b, i, 0)),
            pl.BlockSpec((1, N_MOD, D), lambda b, i: (b, 0, 0)),
            pl.BlockSpec((1, D), const),
            pl.BlockSpec((D, IN_WIDTH), const),
            pl.BlockSpec((tm, V7X_LANES), lambda b, i: (i, 0)),
            pl.BlockSpec((tm, V7X_LANES), lambda b, i: (i, 0)),
        ],
        out_specs=pl.BlockSpec((1, tm, IN_WIDTH), lambda b, i: (b, i, 0)),
        out_shape=jax.ShapeDtypeStruct((B, S, IN_WIDTH), BF16),
        scratch_shapes=[pltpu.VMEM((tm, D), BF16)],
        compiler_params=pltpu.CompilerParams(
            dimension_semantics=("parallel", "parallel"),
            vmem_limit_bytes=_vmem_limit(nbytes)),
        name="inproj",
    )(x, mod, norm_g.reshape(1, D), w_in, cos_tab, sin_tab)


ATT_N_TILES = ATT_TK // ATT_TQ + 1


def _build_bias_tiles(rb_ref, bk_ref, bias_ref, h):
    sb = V7X_LANES
    R = ATT_TK // ATT_TQ
    nk, nq = ATT_TK // sb, ATT_TQ // sb
    far = rb_ref[N_BUCKETS - 1, h]
    near = []
    for t in range(2):
        bk = bk_ref[t]
        tile = jnp.zeros(bk.shape, F32)
        for b in range(N_BUCKETS - 1):
            tile = jnp.where(bk == b, (rb_ref[b, h] - far) * LOG2E, tile)
        near.append(jnp.where(bk < 0, NEG_INF, tile))
    zeros = jnp.zeros((sb, sb), F32)
    masked = jnp.full((sb, sb), NEG_INF, F32)
    for t in range(ATT_N_TILES):
        for ka in range(nk):
            for qa in range(nq):
                d = t * nq + qa - ka if t < R else nk + qa - ka
                blk = masked if d < 0 else near[d] if d < 2 else zeros
                bias_ref[t, ka * sb:(ka + 1) * sb, qa * sb:(qa + 1) * sb] = blk


def _ret_kernel(q_ref, k_ref, v_ref, gate_ref, dec_ref, zeta_ref, xi_ref, gc_ref, o_ref, kv_ref, r_ref, y_ref):
    C = CHUNK
    U = RET_UNROLL
    S = q_ref.shape[1]
    n_chunks = S // C
    lane = lax.broadcasted_iota(jnp.int32, (1, V7X_LANES), 1)
    m_a = jnp.where(lane < RET_DIM, 1.0, 0.0).astype(BF16)
    m_b = jnp.where(lane >= RET_DIM, 1.0, 0.0).astype(BF16)
    row = lax.broadcasted_iota(jnp.int32, (V7X_LANES, V7X_LANES), 0)
    col = lax.broadcasted_iota(jnp.int32, (V7X_LANES, V7X_LANES), 1)
    same_head = jnp.where((row < RET_DIM) == (col < RET_DIM), 1.0, 0.0)
    seg_mean = (same_head * (1.0 / RET_DIM)).astype(BF16)
    gc = gc_ref[0]

    def kv_group(g, carry):
        for u in range(U):
            n = g * U + u
            sl = pl.ds(pl.multiple_of(n * C, C), C)
            kz = (k_ref[0, sl, :].astype(F32) * zeta_ref[0]).T.astype(BF16)
            kv_ref[n] = _dot(kz, v_ref[0, sl, :]) * same_head
        return carry

    lax.fori_loop(0, n_chunks // U, kv_group, 0)

    def scan_step(n, r):
        r_ref[n] = r.astype(BF16)
        return r * gc + kv_ref[n]

    lax.fori_loop(0, n_chunks, scan_step, jnp.zeros((V7X_LANES, V7X_LANES), F32))

    def out_group(g, carry):
        for u in range(U):
            n = g * U + u
            sl = pl.ds(pl.multiple_of(n * C, C), C)
            q = q_ref[0, sl, :]
            k = k_ref[0, sl, :]
            v = v_ref[0, sl, :]
            kk = jnp.concatenate([k * m_a, k * m_b], axis=0)
            vv = jnp.concatenate([v * m_a, v * m_b], axis=0)
            s = _dot_nt(q, kk) * dec_ref[0]
            y_ref[sl, :] = _dot(s.astype(BF16), vv) + _dot(q, r_ref[n]) * xi_ref[0]
        return carry

    lax.fori_loop(0, n_chunks // U, out_group, 0)

    def ln_block(i, carry):
        sl = pl.ds(pl.multiple_of(i * LN_ROWS, LN_ROWS), LN_ROWS)
        y = y_ref[sl, :]
        d = y - _dot(y.astype(BF16), seg_mean)
        var = _dot((d * d).astype(BF16), seg_mean)
        o_ref[0, sl, :] = ((d * lax.rsqrt(var + EPS)) * gate_ref[0, sl, :].astype(F32)).astype(BF16)
        return carry

    lax.fori_loop(0, S // LN_ROWS, ln_block, 0)


def _ret_call(proj, dec, zeta, xi, gc):
    B, S, _ = proj.shape
    n_pairs = RET_HEADS // 2
    blk = lambda off: pl.BlockSpec((1, S, V7X_LANES), lambda b, p: (b, 0, off + p))
    pair = lambda shape: pl.BlockSpec((1,) + shape, lambda b, p: (p, 0, 0))
    return pl.pallas_call(
        _ret_kernel,
        grid=(B, n_pairs),
        in_specs=[
            blk(0), blk(n_pairs), blk(2 * n_pairs), blk(3 * n_pairs),
            pair((CHUNK, 2 * CHUNK)), pair((CHUNK, V7X_LANES)), pair((CHUNK, V7X_LANES)),
            pair((V7X_LANES, V7X_LANES)),
        ],
        out_specs=pl.BlockSpec((1, S, V7X_LANES), lambda b, p: (b, 0, p)),
        out_shape=jax.ShapeDtypeStruct((B, S, RET_WIDTH), BF16),
        scratch_shapes=[
            pltpu.VMEM((S // CHUNK, V7X_LANES, V7X_LANES), F32),
            pltpu.VMEM((S // CHUNK, V7X_LANES, V7X_LANES), BF16),
            pltpu.VMEM((S, V7X_LANES), F32),
        ],
        compiler_params=pltpu.CompilerParams(dimension_semantics=("parallel", "parallel")),
        name="retention",
    )(proj, proj, proj, proj, dec, zeta, xi, gc)


def _attn_steps(S):
    R = ATT_TK // ATT_TQ
    steps = []
    for qi in range(S // ATT_TQ):
        c_diag = qi // R
        for c in range(c_diag + 1):
            n_keys = ATT_TK
            if c == c_diag:
                tile = qi % R
                n_keys = (qi % R + 1) * ATT_TQ
            elif c == c_diag - 1 and qi % R == 0:
                tile = R
            else:
                tile = None
            steps.append((qi, c * ATT_TK, n_keys, tile, c == 0))
    return tuple(steps)


def _attn_kernel(rb_ref, q_ref, k_ref, v_ref, bk_ref, lq1_ref, lk1_ref, lq2_ref, lk2_ref, g_ref, o_ref,
                 bias_ref, vt_ref, sa_ref, sb_ref, ca_ref, cb_ref, m_ref, acc_ref, *, steps):
    TQ, TK = ATT_TQ, ATT_TK
    DV = 2 * DIFF_DIM
    S = q_ref.shape[1]
    lane = lax.broadcasted_iota(jnp.int32, (1, V7X_LANES), 1)
    masks = (jnp.where(lane < DIFF_DIM, 1.0, 0.0).astype(BF16),
             jnp.where(lane >= DIFF_DIM, 1.0, 0.0).astype(BF16))

    @pl.when(pl.program_id(1) == 0)
    def _():
        _build_bias_tiles(rb_ref, bk_ref, bias_ref, pl.program_id(0))

    def transpose_v(c, carry):
        sl = pl.ds(pl.multiple_of(c * TK, TK), TK)
        vt_ref[c, :DV, :] = v_ref[0, sl, :].astype(F32).T.astype(BF16)
        vt_ref[c, DV:, :] = jnp.ones((ATT_ONES_ROWS, TK), BF16)
        return carry

    lax.fori_loop(0, S // TK, transpose_v, 0)

    def score(t, m, s_ref, c_ref):
        qi, k0, nk, tile, _ = steps[t]
        q = q_ref[0, qi * TQ:(qi + 1) * TQ, :]
        s = _dot_nt(k_ref[0, k0:k0 + nk, :], q * masks[m])
        if tile is not None:
            s = s + bias_ref[tile, :nk, :]
        s_ref[m, :nk, :] = s
        c_ref[m] = jnp.max(s, axis=0, keepdims=True)

    def softmax_part(t, m, s_ref, c_ref):
        qi, _, nk, _, restart = steps[t]
        m_new = c_ref[m]
        alpha = None
        if not restart:
            m_old = m_ref[qi, m]
            m_new = jnp.maximum(m_old, m_new)
            alpha = jnp.exp2(m_old - m_new)
        m_ref[qi, m] = m_new
        return jnp.exp2(s_ref[m, :nk, :] - m_new).astype(BF16), alpha

    def value_part(t, m, p, alpha):
        qi, k0, nk, _, restart = steps[t]
        pv = _dot(vt_ref[k0 // TK, :, :nk], p)
        acc_ref[qi, m] = pv if restart else alpha * acc_ref[qi, m] + pv

    buffers = ((sa_ref, ca_ref), (sb_ref, cb_ref))
    score(0, 0, *buffers[0])
    score(0, 1, *buffers[0])
    for t in range(len(steps)):
        cur, nxt = buffers[t % 2], buffers[(t + 1) % 2]
        ahead = t + 1 < len(steps)
        if ahead:
            score(t + 1, 0, *nxt)
        p0, a0 = softmax_part(t, 0, *cur)
        if ahead:
            score(t + 1, 1, *nxt)
        value_part(t, 0, p0, a0)
        p1, a1 = softmax_part(t, 1, *cur)
        value_part(t, 1, p1, a1)

    lam = (jnp.exp(jnp.sum(lq1_ref[...] * lk1_ref[...], axis=-1, keepdims=True))
           - jnp.exp(jnp.sum(lq2_ref[...] * lk2_ref[...], axis=-1, keepdims=True)) + LAMBDA_INIT)

    def finalize(qi, carry):
        a0 = acc_ref[qi, 0]
        a1 = acc_ref[qi, 1]
        o = a0[:DV] / a0[DV:DV + 1] - lam * (a1[:DV] / a1[DV:DV + 1])
        y = o * lax.rsqrt(jnp.mean(o * o, axis=0, keepdims=True) + EPS)
        y = (y * g_ref[...]) * (1.0 - LAMBDA_INIT)
        o_ref[0, pl.ds(pl.multiple_of(qi * TQ, TQ), TQ), :] = y.T.astype(BF16)
        return carry

    lax.fori_loop(0, S // TQ, finalize, 0)


def _attn_call(proj, rel_bias, buckets, lq1, lk1, lq2, lk2, subln_g):
    B, S, _ = proj.shape
    TQ, TK = ATT_TQ, ATT_TK
    n_tiles = ATT_N_TILES
    n_q = S // TQ
    q_off = 4 * RET_WIDTH // V7X_LANES
    rows = 2 * DIFF_DIM + ATT_ONES_ROWS
    nbytes = (8 * S * V7X_LANES * 2 + n_tiles * TK * TQ * 4 + S * rows * 2
              + n_q * 2 * rows * TQ * 4 + 4 * TK * TQ * 4 + 3 * TK * TQ * 4)
    blk = lambda off: pl.BlockSpec((1, S, V7X_LANES), lambda h, b: (b, 0, off + h))
    vec = pl.BlockSpec((1, DIFF_DIM), lambda h, b: (0, 0))
    return pl.pallas_call(
        functools.partial(_attn_kernel, steps=_attn_steps(S)),
        grid=(DIFF_HEADS, B),
        in_specs=[
            pl.BlockSpec(memory_space=pltpu.SMEM),
            blk(q_off), blk(q_off + DIFF_HEADS), blk(q_off + 2 * DIFF_HEADS),
            pl.BlockSpec((2, V7X_LANES, V7X_LANES), lambda h, b: (0, 0, 0)),
            vec, vec, vec, vec,
            pl.BlockSpec((2 * DIFF_DIM, 1), lambda h, b: (0, 0)),
        ],
        out_specs=pl.BlockSpec((1, S, V7X_LANES), lambda h, b: (b, 0, h)),
        out_shape=jax.ShapeDtypeStruct((B, S, DIFF_WIDTH), BF16),
        scratch_shapes=[
            pltpu.VMEM((n_tiles, TK, TQ), F32),
            pltpu.VMEM((S // TK, rows, TK), BF16),
            pltpu.VMEM((2, TK, TQ), F32),
            pltpu.VMEM((2, TK, TQ), F32),
            pltpu.VMEM((2, 1, TQ), F32),
            pltpu.VMEM((2, 1, TQ), F32),
            pltpu.VMEM((n_q, 2, 1, TQ), F32),
            pltpu.VMEM((n_q, 2, rows, TQ), F32),
        ],
        compiler_params=pltpu.CompilerParams(
            dimension_semantics=("arbitrary", "arbitrary"),
            vmem_limit_bytes=_vmem_limit(nbytes)),
        name="diffattn",
    )(rel_bias, proj, proj, proj, buckets, lq1, lk1, lq2, lk2, subln_g.reshape(2 * DIFF_DIM, 1))


def _rope_tables(S):
    half = RET_DIM // 2
    inv = ROPE_BASE ** (-jnp.arange(0, RET_DIM, 2, dtype=F32) / RET_DIM)
    ang = jnp.arange(S).astype(F32)[:, None] * inv[None, :]
    cos = jnp.cos(ang)
    sin = jnp.sin(ang)
    reps = V7X_LANES // RET_DIM
    cos_tab = jnp.tile(jnp.concatenate([cos, cos], axis=1), (1, reps))
    sin_tab = jnp.tile(jnp.concatenate([-sin, sin], axis=1), (1, reps))
    del half
    return cos_tab, sin_tab


def _retention_tables():
    C = CHUNK
    log_gamma = jnp.log1p(-(2.0 ** (-5.0 - jnp.arange(RET_HEADS, dtype=F32))))
    idx = jnp.arange(C, dtype=F32)
    dist = idx[:, None] - idx[None, :]
    decay = jnp.where(dist >= 0, jnp.exp(log_gamma[:, None, None] * jnp.maximum(dist, 0.0)[None]), 0.0)
    zeta = jnp.exp(log_gamma[:, None] * (C - 1 - idx)[None])
    xi = jnp.exp(log_gamma[:, None] * (idx + 1.0)[None])
    gamma_c = jnp.exp(log_gamma * C)
    n_pairs = RET_HEADS // 2
    dec = decay.reshape(n_pairs, 2, C, C).transpose(0, 2, 1, 3).reshape(n_pairs, C, 2 * C)
    lanes = lambda t: jnp.repeat(t.reshape(n_pairs, 2, C).transpose(0, 2, 1), RET_DIM, axis=2)
    g_rows = jnp.repeat(gamma_c.reshape(n_pairs, 2), RET_DIM, axis=1)
    blockdiag = (jnp.arange(V7X_LANES)[:, None] < RET_DIM) == (jnp.arange(V7X_LANES)[None, :] < RET_DIM)
    gc = jnp.where(blockdiag[None], g_rows[:, :, None], 0.0)
    return dec, lanes(zeta), lanes(xi), gc


def _t5_bucket(rel):
    n = jnp.maximum(rel, 0)
    max_exact = N_BUCKETS // 2
    nf = jnp.maximum(n, 1).astype(F32)
    large = max_exact + (jnp.log(nf / max_exact) / math.log(MAX_DIST / max_exact)
                         * (N_BUCKETS - max_exact)).astype(jnp.int32)
    large = jnp.minimum(large, N_BUCKETS - 1)
    return jnp.where(n < max_exact, n, large)


def _bucket_tiles():
    T = V7X_LANES
    kpos = jnp.arange(T)[:, None]
    qpos = jnp.arange(T)[None, :]
    rel0 = qpos - kpos
    rel1 = rel0 + T
    t0 = jnp.where(rel0 >= 0, _t5_bucket(rel0), -1)
    return jnp.stack([t0, _t5_bucket(rel1)]).astype(jnp.int32)


def kernel(x, c, w_ada, b_ada, norm_ffn1, w_ffn1_in, w_ffn1_out, norm_mix, w_in, lambda_q1, lambda_k1,
           lambda_q2, lambda_k2, subln_gain, group_scale, w_out, norm_ffn2, w_ffn2_in, w_ffn2_out, rel_bias,
           norm_final):
    B, S, D = x.shape
    l = 0
    mod = _mod_call(c, w_ada[l], b_ada[l]).reshape(B, N_MOD, D)

    x = _ffn_call(x, mod, norm_ffn1[l], w_ffn1_in[l].astype(BF16), w_ffn1_out[l].astype(BF16), mod_row=0)

    cos_tab, sin_tab = _rope_tables(S)
    proj = _inproj_call(x, mod, norm_mix[l], w_in[l].astype(BF16), cos_tab, sin_tab)
    y_ret = _ret_call(proj, *_retention_tables())
    y_diff = _attn_call(proj, rel_bias, _bucket_tiles(), lambda_q1[l][None], lambda_k1[l][None],
                        lambda_q2[l][None], lambda_k2[l][None], subln_gain[l])

    mixer = (y_ret, y_diff, group_scale[l], w_out[l].astype(BF16), norm_final)
    return _ffn_call(x, mod, norm_ffn2[l], w_ffn2_in[l].astype(BF16), w_ffn2_out[l].astype(BF16), mixer,
                     mod_row=6)
```

```python
import functools
import math

import jax
import jax.numpy as jnp
from jax import lax
from jax.experimental import pallas as pl
from jax.experimental.pallas import tpu as pltpu

D_MODEL = 1024
D_FF = 2816
EPS = 1e-6
RET_DIM = 64
RET_WIDTH = 512
RET_HEADS = 8
DIFF_DIM = 64
DIFF_WIDTH = 512
DIFF_HEADS = 4
IN_WIDTH = 4 * RET_WIDTH + 3 * DIFF_WIDTH
N_BUCKETS = 32
MAX_DIST = 128
CHUNK = 128
ROPE_BASE = 10000.0
N_MOD = 9
NEG_INF = -1e30
LAMBDA_INIT = 0.8 - 0.6 * math.exp(-0.3 * 0)

V7X_LANES = 128
V7X_MXU_DIM = 256
V7X_VMEM_BYTES = 64 * 1024 * 1024

FFN_TM = 1024
FFN_TS = 256
FFN_TF = V7X_MXU_DIM
PROJ_TM = 1024
PROJ_TS = 256
ATT_TQ = 512
ATT_TK = 1024
ATT_ONES_ROWS = 16
RET_UNROLL = 8
LN_ROWS = 1024
LOG2E = math.log2(math.e)
SEG = 512

BF16 = jnp.bfloat16
F32 = jnp.float32


def _vmem_limit(nbytes):
    return int(min(nbytes + (12 << 20), V7X_VMEM_BYTES - (4 << 20)))


def _dot(a, b):
    return jnp.dot(a, b, preferred_element_type=F32)


def _dot_nt(a, b):
    return lax.dot_general(a, b, (((1,), (1,)), ((), ())), preferred_element_type=F32)


def _silu(x):
    return x * (1.0 / (1.0 + jnp.exp(-x)))


def _rms_mod(x, g, shift, scale):
    y = x * lax.rsqrt(jnp.mean(x * x, axis=-1, keepdims=True) + EPS)
    return (y * g) * (1.0 + scale) + shift


def _mod_kernel(c_ref, w_ref, b_ref, o_ref):
    c = c_ref[...]
    o_ref[...] = _dot(_silu(c).astype(BF16), w_ref[...].astype(BF16)) + b_ref[...]


def _mod_call(c, w_ada, b_ada):
    B = c.shape[0]
    n = N_MOD * D_MODEL
    tn = D_MODEL
    return pl.pallas_call(
        _mod_kernel,
        grid=(n // tn,),
        in_specs=[
            pl.BlockSpec((B, D_MODEL), lambda j: (0, 0)),
            pl.BlockSpec((D_MODEL, tn), lambda j: (0, j)),
            pl.BlockSpec((1, tn), lambda j: (0, j)),
        ],
        out_specs=pl.BlockSpec((B, tn), lambda j: (0, j)),
        out_shape=jax.ShapeDtypeStruct((B, n), F32),
        compiler_params=pltpu.CompilerParams(
            dimension_semantics=("parallel",),
            vmem_limit_bytes=_vmem_limit(2 * D_MODEL * tn * 4)),
        name="mod",
    )(c, w_ada, b_ada.reshape(1, n))


def _ffn_kernel(*refs, mod_row, mix):
    if mix:
        (x_ref, mod_ref, g_ref, win_ref, wout_ref, yr_ref, yd_ref, gs_ref, wmix_ref, gf_ref,
         o_ref, h_ref, act_ref) = refs
    else:
        x_ref, mod_ref, g_ref, win_ref, wout_ref, o_ref, h_ref, act_ref = refs
    shift = mod_ref[0, mod_row:mod_row + 1, :]
    scale = mod_ref[0, mod_row + 1:mod_row + 2, :]
    gate = mod_ref[0, mod_row + 2:mod_row + 3, :]
    n_sub = FFN_TM // FFN_TS

    def prologue(sub):
        rows = slice(sub * FFN_TS, (sub + 1) * FFN_TS)
        x = x_ref[0, rows, :]
        if mix:
            gs = gs_ref[...]
            ya = (yr_ref[0, rows, :].astype(F32) * gs[:, :RET_WIDTH]).astype(BF16)
            yb = (yd_ref[0, rows, :].astype(F32) * gs[:, RET_WIDTH:]).astype(BF16)
            x = x + mod_ref[0, 5:6, :] * (_dot(ya, wmix_ref[:RET_WIDTH, :]) + _dot(yb, wmix_ref[RET_WIDTH:, :]))
        h_ref[rows, :] = _rms_mod(x, g_ref[...], shift, scale).astype(BF16)
        return x

    x_next = prologue(0)
    for sub in range(n_sub):
        rows = slice(sub * FFN_TS, (sub + 1) * FFN_TS)
        x = x_next
        if sub + 1 < n_sub:
            x_next = prologue(sub + 1)
        for j in range(D_FF // FFN_TF):
            lo = j * FFN_TF
            g = _dot(h_ref[rows, :], win_ref[:, lo:lo + FFN_TF])
            u = _dot(h_ref[rows, :], win_ref[:, D_FF + lo:D_FF + lo + FFN_TF])
            act_ref[rows, lo:lo + FFN_TF] = (_silu(g) * u).astype(BF16)
        y = x + (0.5 * gate) * _dot(act_ref[rows, :], wout_ref[...])
        if mix:
            y = (y * lax.rsqrt(jnp.mean(y * y, axis=-1, keepdims=True) + EPS)) * gf_ref[...]
        o_ref[0, rows, :] = y


def _ffn_call(x, mod, norm_g, w_in, w_out, mixer=None, *, mod_row):
    B, S, D = x.shape
    tm = FFN_TM
    const = lambda b, i: (0, 0)
    rows = lambda width: pl.BlockSpec((1, tm, width), lambda b, i: (b, i, 0))
    resident = lambda shape: pl.BlockSpec(shape, const, pipeline_mode=pl.Buffered(1))
    in_specs = [
        rows(D),
        pl.BlockSpec((1, N_MOD, D), lambda b, i: (b, 0, 0)),
        pl.BlockSpec((1, D), const),
        resident((D, 2 * D_FF)),
        resident((D_FF, D)),
    ]
    args = [x, mod, norm_g.reshape(1, D), w_in, w_out]
    nbytes = 4 * tm * D * 4 + (D * 2 * D_FF + D_FF * D) * 2 + tm * D * 2 + tm * D_FF * 2
    if mixer is not None:
        y_ret, y_diff, group_scale, w_mix, norm_final = mixer
        in_specs += [rows(RET_WIDTH), rows(DIFF_WIDTH), pl.BlockSpec((1, D), const), resident((D, D)),
                     pl.BlockSpec((1, D), const)]
        args += [y_ret, y_diff, group_scale.reshape(1, D), w_mix, norm_final.reshape(1, D)]
        nbytes += 4 * tm * RET_WIDTH * 2 + D * D * 2
    return pl.pallas_call(
        functools.partial(_ffn_kernel, mod_row=mod_row, mix=mixer is not None),
        grid=(B, S // tm),
        in_specs=in_specs,
        out_specs=rows(D),
        out_shape=jax.ShapeDtypeStruct((B, S, D), F32),
        scratch_shapes=[pltpu.VMEM((tm, D), BF16), pltpu.VMEM((tm, D_FF), BF16)],
        compiler_params=pltpu.CompilerParams(
            dimension_semantics=("parallel", "parallel"),
            vmem_limit_bytes=_vmem_limit(nbytes)),
        name="mix_ffn_final" if mixer is not None else "ffn",
    )(*args)


def _inproj_kernel(x_ref, mod_ref, g_ref, w_ref, cos_ref, sin_ref, o_ref, h_ref):
    lane = lax.broadcasted_iota(jnp.int32, (1, V7X_LANES), 1)
    first_half = (lane % RET_DIM) < (RET_DIM // 2)
    n_sub = PROJ_TM // PROJ_TS

    def prologue(sub):
        rows = slice(sub * PROJ_TS, (sub + 1) * PROJ_TS)
        h_ref[rows, :] = _rms_mod(x_ref[0, rows, :], g_ref[...], mod_ref[0, 3:4, :], mod_ref[0, 4:5, :]).astype(BF16)

    prologue(0)
    for sub in range(n_sub):
        rows = slice(sub * PROJ_TS, (sub + 1) * PROJ_TS)
        if sub + 1 < n_sub:
            prologue(sub + 1)
        cos = cos_ref[rows, :]
        sin = sin_ref[rows, :]
        for seg in range(IN_WIDTH // SEG):
            p = _dot(h_ref[rows, :], w_ref[:, seg * SEG:(seg + 1) * SEG])
            if seg in (0, 1):
                post = RET_DIM ** -0.5 if seg == 1 else 1.0
                for c in range(SEG // V7X_LANES):
                    v = p[:, c * V7X_LANES:(c + 1) * V7X_LANES]
                    rot = jnp.where(first_half,
                                    pltpu.roll(v, V7X_LANES - RET_DIM // 2, axis=1),
                                    pltpu.roll(v, RET_DIM // 2, axis=1))
                    r = v * cos + rot * sin
                    if post != 1.0:
                        r = r * post
                    o_ref[0, rows, seg * SEG + c * V7X_LANES:seg * SEG + (c + 1) * V7X_LANES] = r.astype(BF16)
            elif seg == 3:
                o_ref[0, rows, seg * SEG:(seg + 1) * SEG] = _silu(p).astype(BF16)
            elif seg == 4:
                o_ref[0, rows, seg * SEG:(seg + 1) * SEG] = (p * (LOG2E * DIFF_DIM ** -0.5)).astype(BF16)
            else:
                o_ref[0, rows, seg * SEG:(seg + 1) * SEG] = p.astype(BF16)


def _inproj_call(x, mod, norm_g, w_in, cos_tab, sin_tab):
    B, S, D = x.shape
    tm = PROJ_TM
    const = lambda b, i: (0, 0)
    nbytes = 2 * tm * D * 4 + D * IN_WIDTH * 2 + 2 * tm * IN_WIDTH * 2 + tm * D * 2 + 4 * tm * V7X_LANES * 4
    return pl.pallas_call(
        _inproj_kernel,
        grid=(B, S // tm),
        in_specs=[
            pl.BlockSpec((1, tm, D), lambda b, i: (b, i, 0)),
            pl.BlockSpec((1, N_MOD, D), lambda b, i: (b, 0, 0)),
            pl.BlockSpec((1, D), const),
            pl.BlockSpec((D, IN_WIDTH), const, pipeline_mode=pl.Buffered(1)),
            pl.BlockSpec((tm, V7X_LANES), lambda b, i: (i, 0)),
            pl.BlockSpec((tm, V7X_LANES), lambda b, i: (i, 0)),
        ],
        out_specs=pl.BlockSpec((1, tm, IN_WIDTH), lambda b, i: (b, i, 0)),
        out_shape=jax.ShapeDtypeStruct((B, S, IN_WIDTH), BF16),
        scratch_shapes=[pltpu.VMEM((tm, D), BF16)],
        compiler_params=pltpu.CompilerParams(
            dimension_semantics=("parallel", "parallel"),
            vmem_limit_bytes=_vmem_limit(nbytes)),
        name="inproj",
    )(x, mod, norm_g.reshape(1, D), w_in, cos_tab, sin_tab)


ATT_N_TILES = ATT_TK // ATT_TQ + 1


def _build_bias_tiles(rb_ref, bk_ref, bias_ref, h):
    sb = V7X_LANES
    R = ATT_TK // ATT_TQ
    nk, nq = ATT_TK // sb, ATT_TQ // sb
    far = rb_ref[N_BUCKETS - 1, h]
    near = []
    for t in range(2):
        bk = bk_ref[t]
        tile = jnp.zeros(bk.shape, F32)
        for b in range(N_BUCKETS - 1):
            tile = jnp.where(bk == b, (rb_ref[b, h] - far) * LOG2E, tile)
        near.append(jnp.where(bk < 0, NEG_INF, tile))
    zeros = jnp.zeros((sb, sb), F32)
    masked = jnp.full((sb, sb), NEG_INF, F32)
    for t in range(ATT_N_TILES):
        for ka in range(nk):
            for qa in range(nq):
                d = t * nq + qa - ka if t < R else nk + qa - ka
                blk = masked if d < 0 else near[d] if d < 2 else zeros
                bias_ref[t, ka * sb:(ka + 1) * sb, qa * sb:(qa + 1) * sb] = blk


def _ret_kernel(q_ref, k_ref, v_ref, gate_ref, dec_ref, zeta_ref, xi_ref, gc_ref, o_ref, kv_ref, r_ref, y_ref):
    C = CHUNK
    U = RET_UNROLL
    S = q_ref.shape[1]
    n_chunks = S // C
    lane = lax.broadcasted_iota(jnp.int32, (1, V7X_LANES), 1)
    m_a = jnp.where(lane < RET_DIM, 1.0, 0.0).astype(BF16)
    m_b = jnp.where(lane >= RET_DIM, 1.0, 0.0).astype(BF16)
    row = lax.broadcasted_iota(jnp.int32, (V7X_LANES, V7X_LANES), 0)
    col = lax.broadcasted_iota(jnp.int32, (V7X_LANES, V7X_LANES), 1)
    same_head = jnp.where((row < RET_DIM) == (col < RET_DIM), 1.0, 0.0)
    seg_mean = (same_head * (1.0 / RET_DIM)).astype(BF16)
    gc = gc_ref[0]

    def kv_group(g, carry):
        for u in range(U):
            n = g * U + u
            sl = pl.ds(pl.multiple_of(n * C, C), C)
            kz = (k_ref[0, sl, :].astype(F32) * zeta_ref[0]).T.astype(BF16)
            kv_ref[n] = _dot(kz, v_ref[0, sl, :]) * same_head
        return carry

    lax.fori_loop(0, n_chunks // U, kv_group, 0)

    def scan_step(n, r):
        r_ref[n] = r.astype(BF16)
        return r * gc + kv_ref[n]

    lax.fori_loop(0, n_chunks, scan_step, jnp.zeros((V7X_LANES, V7X_LANES), F32))

    def out_group(g, carry):
        for u in range(U):
            n = g * U + u
            sl = pl.ds(pl.multiple_of(n * C, C), C)
            q = q_ref[0, sl, :]
            k = k_ref[0, sl, :]
            v = v_ref[0, sl, :]
            kk = jnp.concatenate([k * m_a, k * m_b], axis=0)
            vv = jnp.concatenate([v * m_a, v * m_b], axis=0)
            s = _dot_nt(q, kk) * dec_ref[0]
            y_ref[sl, :] = _dot(s.astype(BF16), vv) + _dot(q, r_ref[n]) * xi_ref[0]
        return carry

    lax.fori_loop(0, n_chunks // U, out_group, 0)

    def ln_block(i, carry):
        sl = pl.ds(pl.multiple_of(i * LN_ROWS, LN_ROWS), LN_ROWS)
        y = y_ref[sl, :]
        d = y - _dot(y.astype(BF16), seg_mean)
        var = _dot((d * d).astype(BF16), seg_mean)
        o_ref[0, sl, :] = ((d * lax.rsqrt(var + EPS)) * gate_ref[0, sl, :].astype(F32)).astype(BF16)
        return carry

    lax.fori_loop(0, S // LN_ROWS, ln_block, 0)


def _ret_call(proj, dec, zeta, xi, gc):
    B, S, _ = proj.shape
    n_pairs = RET_HEADS // 2
    blk = lambda off: pl.BlockSpec((1, S, V7X_LANES), lambda b, p: (b, 0, off + p))
    pair = lambda shape: pl.BlockSpec((1,) + shape, lambda b, p: (p, 0, 0))
    return pl.pallas_call(
        _ret_kernel,
        grid=(B, n_pairs),
        in_specs=[
            blk(0), blk(n_pairs), blk(2 * n_pairs), blk(3 * n_pairs),
            pair((CHUNK, 2 * CHUNK)), pair((CHUNK, V7X_LANES)), pair((CHUNK, V7X_LANES)),
            pair((V7X_LANES, V7X_LANES)),
        ],
        out_specs=pl.BlockSpec((1, S, V7X_LANES), lambda b, p: (b, 0, p)),
        out_shape=jax.ShapeDtypeStruct((B, S, RET_WIDTH), BF16),
        scratch_shapes=[
            pltpu.VMEM((S // CHUNK, V7X_LANES, V7X_LANES), F32),
            pltpu.VMEM((S // CHUNK, V7X_LANES, V7X_LANES), BF16),
            pltpu.VMEM((S, V7X_LANES), F32),
        ],
        compiler_params=pltpu.CompilerParams(dimension_semantics=("parallel", "parallel")),
        name="retention",
    )(proj, proj, proj, proj, dec, zeta, xi, gc)


def _attn_steps(S):
    R = ATT_TK // ATT_TQ
    steps = []
    for qi in range(S // ATT_TQ):
        c_diag = qi // R
        for c in range(c_diag + 1):
            n_keys = ATT_TK
            if c == c_diag:
                tile = qi % R
                n_keys = (qi % R + 1) * ATT_TQ
            elif c == c_diag - 1 and qi % R == 0:
                tile = R
            else:
                tile = None
            steps.append((qi, c * ATT_TK, n_keys, tile, c == 0))
    return tuple(steps)


def _attn_kernel(rb_ref, q_ref, k_ref, v_ref, bk_ref, lq1_ref, lk1_ref, lq2_ref, lk2_ref, g_ref, o_ref,
                 bias_ref, vt_ref, sa_ref, sb_ref, ca_ref, cb_ref, m_ref, acc_ref, *, steps):
    TQ, TK = ATT_TQ, ATT_TK
    DV = 2 * DIFF_DIM
    S = q_ref.shape[1]
    lane = lax.broadcasted_iota(jnp.int32, (1, V7X_LANES), 1)
    masks = (jnp.where(lane < DIFF_DIM, 1.0, 0.0).astype(BF16),
             jnp.where(lane >= DIFF_DIM, 1.0, 0.0).astype(BF16))

    @pl.when(pl.program_id(1) == 0)
    def _():
        _build_bias_tiles(rb_ref, bk_ref, bias_ref, pl.program_id(0))

    def transpose_v(c, carry):
        sl = pl.ds(pl.multiple_of(c * TK, TK), TK)
        vt_ref[c, :DV, :] = v_ref[0, sl, :].astype(F32).T.astype(BF16)
        vt_ref[c, DV:, :] = jnp.ones((ATT_ONES_ROWS, TK), BF16)
        return carry

    lax.fori_loop(0, S // TK, transpose_v, 0)

    def score(t, m, s_ref, c_ref):
        qi, k0, nk, tile, _ = steps[t]
        q = q_ref[0, qi * TQ:(qi + 1) * TQ, :]
        s = _dot_nt(k_ref[0, k0:k0 + nk, :], q * masks[m])
        if tile is not None:
            s = s + bias_ref[tile, :nk, :]
        s_ref[m, :nk, :] = s
        c_ref[m] = jnp.max(s, axis=0, keepdims=True)

    def softmax_part(t, m, s_ref, c_ref):
        qi, _, nk, _, restart = steps[t]
        m_new = c_ref[m]
        alpha = None
        if not restart:
            m_old = m_ref[qi, m]
            m_new = jnp.maximum(m_old, m_new)
            alpha = jnp.exp2(m_old - m_new)
        m_ref[qi, m] = m_new
        return jnp.exp2(s_ref[m, :nk, :] - m_new).astype(BF16), alpha

    def value_part(t, m, p, alpha):
        qi, k0, nk, _, restart = steps[t]
        pv = _dot(vt_ref[k0 // TK, :, :nk], p)
        acc_ref[qi, m] = pv if restart else alpha * acc_ref[qi, m] + pv

    buffers = ((sa_ref, ca_ref), (sb_ref, cb_ref))
    score(0, 0, *buffers[0])
    score(0, 1, *buffers[0])
    for t in range(len(steps)):
        cur, nxt = buffers[t % 2], buffers[(t + 1) % 2]
        ahead = t + 1 < len(steps)
        if ahead:
            score(t + 1, 0, *nxt)
        p0, a0 = softmax_part(t, 0, *cur)
        value_part(t, 0, p0, a0)
        if ahead:
            score(t + 1, 1, *nxt)
        p1, a1 = softmax_part(t, 1, *cur)
        value_part(t, 1, p1, a1)

    lam = (jnp.exp(jnp.sum(lq1_ref[...] * lk1_ref[...], axis=-1, keepdims=True))
           - jnp.exp(jnp.sum(lq2_ref[...] * lk2_ref[...], axis=-1, keepdims=True)) + LAMBDA_INIT)

    def finalize(qi, carry):
        a0 = acc_ref[qi, 0]
        a1 = acc_ref[qi, 1]
        o = a0[:DV] / a0[DV:DV + 1] - lam * (a1[:DV] / a1[DV:DV + 1])
        y = o * lax.rsqrt(jnp.mean(o * o, axis=0, keepdims=True) + EPS)
        y = (y * g_ref[...]) * (1.0 - LAMBDA_INIT)
        o_ref[0, pl.ds(pl.multiple_of(qi * TQ, TQ), TQ), :] = y.T.astype(BF16)
        return carry

    lax.fori_loop(0, S // TQ, finalize, 0)


def _attn_call(proj, rel_bias, buckets, lq1, lk1, lq2, lk2, subln_g):
    B, S, _ = proj.shape
    TQ, TK = ATT_TQ, ATT_TK
    n_tiles = ATT_N_TILES
    n_q = S // TQ
    q_off = 4 * RET_WIDTH // V7X_LANES
    rows = 2 * DIFF_DIM + ATT_ONES_ROWS
    nbytes = (8 * S * V7X_LANES * 2 + n_tiles * TK * TQ * 4 + S * rows * 2
              + n_q * 2 * rows * TQ * 4 + 4 * TK * TQ * 4 + 3 * TK * TQ * 4)
    blk = lambda off: pl.BlockSpec((1, S, V7X_LANES), lambda h, b: (b, 0, off + h))
    vec = pl.BlockSpec((1, DIFF_DIM), lambda h, b: (0, 0))
    return pl.pallas_call(
        functools.partial(_attn_kernel, steps=_attn_steps(S)),
        grid=(DIFF_HEADS, B),
        in_specs=[
            pl.BlockSpec(memory_space=pltpu.SMEM),
            blk(q_off), blk(q_off + DIFF_HEADS), blk(q_off + 2 * DIFF_HEADS),
            pl.BlockSpec((2, V7X_LANES, V7X_LANES), lambda h, b: (0, 0, 0)),
            vec, vec, vec, vec,
            pl.BlockSpec((2 * DIFF_DIM, 1), lambda h, b: (0, 0)),
        ],
        out_specs=pl.BlockSpec((1, S, V7X_LANES), lambda h, b: (b, 0, h)),
        out_shape=jax.ShapeDtypeStruct((B, S, DIFF_WIDTH), BF16),
        scratch_shapes=[
            pltpu.VMEM((n_tiles, TK, TQ), F32),
            pltpu.VMEM((S // TK, rows, TK), BF16),
            pltpu.VMEM((2, TK, TQ), F32),
            pltpu.VMEM((2, TK, TQ), F32),
            pltpu.VMEM((2, 1, TQ), F32),
            pltpu.VMEM((2, 1, TQ), F32),
            pltpu.VMEM((n_q, 2, 1, TQ), F32),
            pltpu.VMEM((n_q, 2, rows, TQ), F32),
        ],
        compiler_params=pltpu.CompilerParams(
            dimension_semantics=("arbitrary", "arbitrary"),
            vmem_limit_bytes=_vmem_limit(nbytes)),
        name="diffattn",
    )(rel_bias, proj, proj, proj, buckets, lq1, lk1, lq2, lk2, subln_g.reshape(2 * DIFF_DIM, 1))


def _rope_tables(S):
    inv = ROPE_BASE ** (-jnp.arange(0, RET_DIM, 2, dtype=F32) / RET_DIM)
    ang = jnp.arange(S).astype(F32)[:, None] * inv[None, :]
    cos = jnp.cos(ang)
    sin = jnp.sin(ang)
    reps = V7X_LANES // RET_DIM
    cos_tab = jnp.tile(jnp.concatenate([cos, cos], axis=1), (1, reps))
    sin_tab = jnp.tile(jnp.concatenate([-sin, sin], axis=1), (1, reps))
    return cos_tab, sin_tab


def _retention_tables():
    C = CHUNK
    log_gamma = jnp.log1p(-(2.0 ** (-5.0 - jnp.arange(RET_HEADS, dtype=F32))))
    idx = jnp.arange(C, dtype=F32)
    dist = idx[:, None] - idx[None, :]
    decay = jnp.where(dist >= 0, jnp.exp(log_gamma[:, None, None] * jnp.maximum(dist, 0.0)[None]), 0.0)
    zeta = jnp.exp(log_gamma[:, None] * (C - 1 - idx)[None])
    xi = jnp.exp(log_gamma[:, None] * (idx + 1.0)[None])
    gamma_c = jnp.exp(log_gamma * C)
    n_pairs = RET_HEADS // 2
    dec = decay.reshape(n_pairs, 2, C, C).transpose(0, 2, 1, 3).reshape(n_pairs, C, 2 * C)
    lanes = lambda t: jnp.repeat(t.reshape(n_pairs, 2, C).transpose(0, 2, 1), RET_DIM, axis=2)
    g_rows = jnp.repeat(gamma_c.reshape(n_pairs, 2), RET_DIM, axis=1)
    blockdiag = (jnp.arange(V7X_LANES)[:, None] < RET_DIM) == (jnp.arange(V7X_LANES)[None, :] < RET_DIM)
    gc = jnp.where(blockdiag[None], g_rows[:, :, None], 0.0)
    return dec, lanes(zeta), lanes(xi), gc


def _t5_bucket(rel):
    n = jnp.maximum(rel, 0)
    max_exact = N_BUCKETS // 2
    nf = jnp.maximum(n, 1).astype(F32)
    large = max_exact + (jnp.log(nf / max_exact) / math.log(MAX_DIST / max_exact)
                         * (N_BUCKETS - max_exact)).astype(jnp.int32)
    large = jnp.minimum(large, N_BUCKETS - 1)
    return jnp.where(n < max_exact, n, large)


def _bucket_tiles():
    T = V7X_LANES
    kpos = jnp.arange(T)[:, None]
    qpos = jnp.arange(T)[None, :]
    rel0 = qpos - kpos
    rel1 = rel0 + T
    t0 = jnp.where(rel0 >= 0, _t5_bucket(rel0), -1)
    return jnp.stack([t0, _t5_bucket(rel1)]).astype(jnp.int32)


def kernel(x, c, w_ada, b_ada, norm_ffn1, w_ffn1_in, w_ffn1_out, norm_mix, w_in, lambda_q1, lambda_k1,
           lambda_q2, lambda_k2, subln_gain, group_scale, w_out, norm_ffn2, w_ffn2_in, w_ffn2_out, rel_bias,
           norm_final):
    B, S, D = x.shape
    l = 0
    mod = _mod_call(c, w_ada[l], b_ada[l]).reshape(B, N_MOD, D)

    x = _ffn_call(x, mod, norm_ffn1[l], w_ffn1_in[l].astype(BF16), w_ffn1_out[l].astype(BF16), mod_row=0)

    cos_tab, sin_tab = _rope_tables(S)
    proj = _inproj_call(x, mod, norm_mix[l], w_in[l].astype(BF16), cos_tab, sin_tab)
    y_ret = _ret_call(proj, *_retention_tables())
    y_diff = _attn_call(proj, rel_bias, _bucket_tiles(), lambda_q1[l][None], lambda_k1[l][None],
                        lambda_q2[l][None], lambda_k2[l][None], subln_gain[l])

    mixer = (y_ret, y_diff, group_scale[l], w_out[l].astype(BF16), norm_final)
    return _ffn_call(x, mod, norm_ffn2[l], w_ffn2_in[l].astype(BF16), w_ffn2_out[l].astype(BF16), mixer,
                     mod_row=6)
```
